```python
import jax
import jax.numpy as jnp
from jax import lax
import numpy as np

D_MODEL = 1024
BATCH = 4
SEQ = 8192
DEPTH = 2

N_GROUPS = 5
HEAD_DIM = 64
GROUP_HEADS = 4
GROUP_W = GROUP_HEADS * HEAD_DIM
MIX_W = N_GROUPS * GROUP_W
N_MEM = 256
CHUNK = 128
CONV_K = 4
RWKV_DECAY_RANK = 16
RWKV_A_RANK = 16
EPS = 1e-6

A_COLS = (GROUP_W, GROUP_W, GROUP_W)
B_COLS = (GROUP_W, GROUP_W, GROUP_W, GROUP_HEADS, GROUP_W)
C_COLS = (2 * GROUP_W, GROUP_W, GROUP_HEADS, GROUP_HEADS, GROUP_W, GROUP_W)
D_COLS = (GROUP_W, RWKV_DECAY_RANK, GROUP_W, GROUP_W, RWKV_A_RANK, GROUP_W)
M_COLS = (GROUP_W, GROUP_W)
GROUP_COLS = (sum(A_COLS), sum(B_COLS), sum(C_COLS), sum(D_COLS), sum(M_COLS))
D_W = sum(D_COLS)
IN_COLS = sum(GROUP_COLS)

kernel_name = "hybrid_parallel_groups_gmlp_fox_mlstm_rwkv7_mem"


def rms_norm(x, g):
    xf = x.astype(jnp.float32)
    y = xf * lax.rsqrt(jnp.mean(xf * xf, axis=-1, keepdims=True) + EPS)
    return (y * g.astype(jnp.float32)).astype(x.dtype)


def split_cols(x, widths):
    return jnp.split(x, np.cumsum(widths)[:-1].tolist(), axis=-1)


def to_heads(t):
    return t.reshape(t.shape[:-1] + (GROUP_HEADS, HEAD_DIM))


def causal_dwconv(x, w, b):
    y = lax.conv_general_dilated(
        x, w[:, None, :].astype(x.dtype), (1,), [(CONV_K - 1, 0)],
        dimension_numbers=("NWC", "WIO", "NWC"), feature_group_count=x.shape[-1])
    return y + b


def token_shift(x, mu):
    prev = jnp.pad(x, ((0, 0), (1, 0), (0, 0)))[:, :-1]
    return x + mu * (prev - x)


def spatial_gating(u, v, norm_g, w_s, b_s):
    bsz, seq, _ = u.shape
    nc = seq // CHUNK
    u = jax.nn.gelu(u)
    v = rms_norm(to_heads(jax.nn.gelu(v)), norm_g.reshape(GROUP_HEADS, HEAD_DIM))
    v = v.reshape(bsz, nc, CHUNK, GROUP_HEADS, HEAD_DIM)
    w = jnp.where(jnp.tril(jnp.ones((CHUNK, CHUNK), bool)), w_s, jnp.zeros_like(w_s))
    mixed = jnp.einsum("gts,bcsgd->bctgd", w, v) + b_s.T[:, :, None]
    return u * mixed.reshape(bsz, seq, GROUP_W)


def forgetting_attention(q, k, v, f_logit, q_g, k_g):
    bsz, seq, _ = q.shape
    nb = seq // CHUNK
    q = rms_norm(to_heads(q), q_g)
    k = rms_norm(to_heads(k), k_g)
    v = to_heads(v)
    big_f = jnp.cumsum(jax.nn.log_sigmoid(f_logit.astype(jnp.float32)), axis=1)
    big_f = big_f.transpose(0, 2, 1)
    q_blocks = q.reshape(bsz, nb, CHUNK, GROUP_HEADS, HEAD_DIM).swapaxes(0, 1)
    f_blocks = big_f.reshape(bsz, GROUP_HEADS, nb, CHUNK).transpose(2, 0, 1, 3)
    k_pos = jnp.arange(seq)
    scale = HEAD_DIM ** -0.5

    def one_block(args):
        qb, fb, start = args
        s = jnp.einsum("bthd,bshd->bhts", qb, k).astype(jnp.float32) * scale
        s = s + fb[..., :, None] - big_f[..., None, :]
        q_pos = start + jnp.arange(CHUNK)
        s = jnp.where(k_pos[None, :] <= q_pos[:, None], s, -jnp.inf)
        p = jax.nn.softmax(s, axis=-1).astype(v.dtype)
        return jnp.einsum("bhts,bshd->bthd", p, v)

    out = lax.map(one_block, (q_blocks, f_blocks, jnp.arange(nb) * CHUNK))
    return out.swapaxes(0, 1).reshape(bsz, seq, GROUP_W)


def mlstm(qk, v, i_logit, f_logit, o_logit, conv_w, conv_b, out_g):
    f32 = jnp.float32
    bsz, seq, _ = v.shape
    nc = seq // CHUNK
    q, k = jnp.split(jax.nn.silu(causal_dwconv(qk, conv_w, conv_b)), 2, axis=-1)
    q = to_heads(q).astype(f32)
    k = to_heads(k).astype(f32) * HEAD_DIM ** -0.5
    vh = to_heads(v).astype(f32)
    log_i = i_logit.astype(f32)
    log_f = jax.nn.log_sigmoid(f_logit.astype(f32))

    def chunks(t):
        return t.reshape((bsz, nc, CHUNK) + t.shape[2:]).swapaxes(0, 1)

    causal = jnp.tril(jnp.ones((CHUNK, CHUNK), bool))

    def step(carry, inp):
        c_state, n_state, m_state = carry
        qc, kc, vc, ic, fc = inp
        b = jnp.cumsum(fc, axis=1).swapaxes(1, 2)
        ic = ic.swapaxes(1, 2)
        g = b[..., -1]
        log_d = jnp.where(causal, b[..., :, None] - b[..., None, :] + ic[..., None, :], -jnp.inf)
        inter = b + m_state[..., None]
        m_t = jnp.maximum(jnp.max(log_d, axis=-1), inter)
        s = jnp.einsum("bthd,bshd->bhts", qc, kc) * jnp.exp(log_d - m_t[..., None])
        w_inter = jnp.exp(inter - m_t)
        num = (jnp.einsum("bhts,bshd->bhtd", s, vc)
               + w_inter[..., None] * jnp.einsum("bthk,bhkd->bhtd", qc, c_state))
        den = jnp.sum(s, axis=-1) + w_inter * jnp.einsum("bthk,bhk->bht", qc, n_state)
        h = num / jnp.maximum(jnp.abs(den), jnp.exp(-m_t))[..., None]
        log_w = g[..., None] - b + ic
        m_new = jnp.maximum(g + m_state, jnp.max(log_w, axis=-1))
        w_new = jnp.exp(log_w - m_new[..., None])
        carry_decay = jnp.exp(g + m_state - m_new)
        c_state = carry_decay[..., None, None] * c_state + jnp.einsum("bhs,bshk,bshd->bhkd", w_new, kc, vc)
        n_state = carry_decay[..., None] * n_state + jnp.einsum("bhs,bshk->bhk", w_new, kc)
        return (c_state, n_state, m_new), h.swapaxes(1, 2)

    init = (jnp.zeros((bsz, GROUP_HEADS, HEAD_DIM, HEAD_DIM), f32),
            jnp.zeros((bsz, GROUP_HEADS, HEAD_DIM), f32),
            jnp.zeros((bsz, GROUP_HEADS), f32))
    _, h = lax.scan(step, init, (chunks(q), chunks(k), chunks(vh), chunks(log_i), chunks(log_f)))
    h = h.swapaxes(0, 1).reshape(bsz, seq, GROUP_HEADS, HEAD_DIM)
    h = jax.nn.sigmoid(to_heads(o_logit).astype(f32)) * h
    return rms_norm(h, out_g.reshape(GROUP_HEADS, HEAD_DIM)).reshape(bsz, seq, GROUP_W)


def rwkv7_time_mix(r, w_lo, k, v, a_lo, w0, w2, a0, a2, k_k, k_a, r_k, ln_g):
    f32 = jnp.float32
    bsz, seq, _ = r.shape
    r, w_lo, k, v, a_lo = (t.astype(f32) for t in (r, w_lo, k, v, a_lo))
    w_log = -jax.nn.softplus(-(w0 + jnp.tanh(w_lo) @ w2)) - 0.5
    decay = jnp.exp(-jnp.exp(w_log))
    a = jax.nn.sigmoid(a0 + a_lo @ a2)
    kk = to_heads(k * k_k)
    kk = kk / jnp.maximum(jnp.linalg.norm(kk, axis=-1, keepdims=True), 1e-12)
    k = k * (1 + (a - 1) * k_a)
    rh, dh, kh, vh, ah = (to_heads(t) for t in (r, decay, k, v, a))

    def step(state, inp):
        r_t, w_t, k_t, v_t, kk_t, a_t = inp
        sa = jnp.einsum("bhvk,bhk->bhv", state, -kk_t)
        state = (state * w_t[:, :, None, :] + sa[..., None] * (kk_t * a_t)[:, :, None, :]
                 + v_t[..., None] * k_t[:, :, None, :])
        return state, jnp.einsum("bhvk,bhk->bhv", state, r_t)

    init = jnp.zeros((bsz, GROUP_HEADS, HEAD_DIM, HEAD_DIM), f32)
    _, y = lax.scan(step, init, tuple(t.swapaxes(0, 1) for t in (rh, dh, kh, vh, kk, ah)))
    y = rms_norm(y.swapaxes(0, 1), ln_g.reshape(GROUP_HEADS, HEAD_DIM))
    y = y + jnp.sum(rh * kh * r_k, axis=-1, keepdims=True) * vh
    return y.reshape(bsz, seq, GROUP_W)


def memory_attention(q, mem_n, w_kv, q_g, k_g):
    bsz, seq, _ = q.shape
    k, v = jnp.split(mem_n @ w_kv, 2, axis=-1)
    q = rms_norm(to_heads(q), q_g)
    k = rms_norm(to_heads(k), k_g)
    v = to_heads(v)
    s = jnp.einsum("bthd,bmhd->bhtm", q, k).astype(jnp.float32) * HEAD_DIM ** -0.5
    p = jax.nn.softmax(s, axis=-1).astype(v.dtype)
    return jnp.einsum("bhtm,bmhd->bthd", p, v).reshape(bsz, seq, GROUP_W)


def hybrid_layer(x, mem, norm_g, w_in, w_out, sgu_norm_g, sgu_w, sgu_b,
                 fox_q_g, fox_k_g, fox_f_b,
                 mlstm_conv_w, mlstm_conv_b, mlstm_i_b, mlstm_f_b, mlstm_out_g,
                 rwkv_mu, rwkv_w0, rwkv_w2, rwkv_a0, rwkv_a2, rwkv_k_k, rwkv_k_a, rwkv_r_k, rwkv_ln_g,
                 mem_norm_g, mem_w_kv, mem_q_g, mem_k_g):
    h = rms_norm(x, norm_g)
    proj = h @ w_in
    pa, pb, pc, pd, pm = split_cols(proj, GROUP_COLS)

    a_u, a_v, a_z = split_cols(pa, A_COLS)
    ya = spatial_gating(a_u, a_v, sgu_norm_g, sgu_w, sgu_b)

    b_q, b_k, b_v, b_f, b_z = split_cols(pb, B_COLS)
    yb = forgetting_attention(b_q, b_k, b_v, b_f + fox_f_b, fox_q_g, fox_k_g)

    c_qk, c_v, c_i, c_f, c_o, c_z = split_cols(pc, C_COLS)
    yc = mlstm(c_qk, c_v, c_i + mlstm_i_b, c_f + mlstm_f_b, c_o, mlstm_conv_w, mlstm_conv_b, mlstm_out_g)

    d_r, d_wlo, d_k, d_v, d_alo, d_z = split_cols(token_shift(pd, rwkv_mu), D_COLS)
    yd = rwkv7_time_mix(d_r, d_wlo, d_k, d_v, d_alo, rwkv_w0, rwkv_w2, rwkv_a0, rwkv_a2,
                        rwkv_k_k, rwkv_k_a, rwkv_r_k, rwkv_ln_g)

    m_q, m_z = split_cols(pm, M_COLS)
    ym = memory_attention(m_q, rms_norm(mem, mem_norm_g), mem_w_kv, mem_q_g, mem_k_g)

    branches = [(ya, a_z), (yb, b_z), (yc, c_z), (yd, d_z), (ym, m_z)]
    mixed = jnp.concatenate([y.astype(x.dtype) * jax.nn.silu(z) for y, z in branches], axis=-1)
    return x + mixed @ w_out


def setup_inputs(seed: int = 0) -> dict:
    key = jax.random.key(seed)
    ks = iter(jax.random.split(key, 32))
    f32 = jnp.float32

    def nrm(shape, scale):
        return jax.random.normal(next(ks), shape, f32) * scale

    def uni(shape, lo, hi):
        return jax.random.uniform(next(ks), shape, f32, lo, hi)

    def gain(shape):
        return 1.0 + nrm(shape, 0.1)

    L = DEPTH
    return {
        "x": nrm((BATCH, SEQ, D_MODEL), 1.0),
        "mem": nrm((BATCH, N_MEM, D_MODEL), 1.0),
        "norm_g": gain((L, D_MODEL)),
        "w_in": nrm((L, D_MODEL, IN_COLS), D_MODEL ** -0.5),
        "w_out": nrm((L, MIX_W, D_MODEL), MIX_W ** -0.5),
        "sgu_norm_g": gain((L, GROUP_W)),
        "sgu_w": nrm((L, GROUP_HEADS, CHUNK, CHUNK), 0.5 * CHUNK ** -0.5),
        "sgu_b": gain((L, GROUP_HEADS, CHUNK)),
        "fox_q_g": gain((L, HEAD_DIM)),
        "fox_k_g": gain((L, HEAD_DIM)),
        "fox_f_b": uni((L, GROUP_HEADS), 1.0, 5.0),
        "mlstm_conv_w": nrm((L, CONV_K, 2 * GROUP_W), CONV_K ** -0.5),
        "mlstm_conv_b": nrm((L, 2 * GROUP_W), 0.02),
        "mlstm_i_b": nrm((L, GROUP_HEADS), 0.1),
        "mlstm_f_b": uni((L, GROUP_HEADS), 3.0, 6.0),
        "mlstm_out_g": gain((L, GROUP_W)),
        "rwkv_mu": uni((L, D_W), 0.0, 1.0),
        "rwkv_w0": uni((L, GROUP_W), -4.0, 0.0),
        "rwkv_w2": nrm((L, RWKV_DECAY_RANK, GROUP_W), 0.5 * RWKV_DECAY_RANK ** -0.5),
        "rwkv_a0": nrm((L, GROUP_W), 0.1),
        "rwkv_a2": nrm((L, RWKV_A_RANK, GROUP_W), 0.5 * RWKV_A_RANK ** -0.5),
        "rwkv_k_k": 0.85 + nrm((L, GROUP_W), 0.05),
        "rwkv_k_a": gain((L, GROUP_W)),
        "rwkv_r_k": nrm((L, GROUP_HEADS, HEAD_DIM), 0.1),
        "rwkv_ln_g": gain((L, GROUP_W)),
        "mem_norm_g": gain((L, D_MODEL)),
        "mem_w_kv": nrm((L, D_MODEL, 2 * GROUP_W), D_MODEL ** -0.5),
        "mem_q_g": gain((L, HEAD_DIM)),
        "mem_k_g": gain((L, HEAD_DIM)),
    }


def reference(x, mem, norm_g, w_in, w_out, sgu_norm_g, sgu_w, sgu_b,
              fox_q_g, fox_k_g, fox_f_b,
              mlstm_conv_w, mlstm_conv_b, mlstm_i_b, mlstm_f_b, mlstm_out_g,
              rwkv_mu, rwkv_w0, rwkv_w2, rwkv_a0, rwkv_a2, rwkv_k_k, rwkv_k_a, rwkv_r_k, rwkv_ln_g,
              mem_norm_g, mem_w_kv, mem_q_g, mem_k_g):
    stacked = (norm_g, w_in, w_out, sgu_norm_g, sgu_w, sgu_b,
               fox_q_g, fox_k_g, fox_f_b,
               mlstm_conv_w, mlstm_conv_b, mlstm_i_b, mlstm_f_b, mlstm_out_g,
               rwkv_mu, rwkv_w0, rwkv_w2, rwkv_a0, rwkv_a2, rwkv_k_k, rwkv_k_a, rwkv_r_k, rwkv_ln_g,
               mem_norm_g, mem_w_kv, mem_q_g, mem_k_g)
    for layer in range(DEPTH):
        x = hybrid_layer(x, mem, *[p[layer] for p in stacked])
    return x
```

```python
import functools

import jax
import jax.numpy as jnp
from jax import lax
from jax.experimental import pallas as pl
from jax.experimental.pallas import tpu as pltpu

F32 = jnp.float32
BF16 = jnp.bfloat16

HEAD_DIM = 64
GROUP_HEADS = 4
GROUP_W = GROUP_HEADS * HEAD_DIM
LANES = 128
CHUNK = 128
CONV_K = 4
EPS = 1e-6
SCALE = HEAD_DIM ** -0.5

CB = dict(a_u=0, a_v=1, a_z=2, b_q=3, b_k=4, b_v=5, b_z=6, c_q=7, c_k=8, c_v=9, c_o=10, c_z=11,
          d_r=12, d_k=13, d_v=14, d_z=15, m_q=16, m_z=17)
N_WIDE = 18 * GROUP_W
SMALL_CB = N_WIDE // LANES
NP = N_WIDE + LANES
SM_BF, SM_CI, SM_CF, SM_WLO, SM_ALO = 0, 4, 8, 16, 32
DECAY_RANK = 16
A_RANK = 16

VMEM_LIMIT = 48 * 1024 * 1024

NN = (((1,), (0,)), ((), ()))
NT = (((1,), (1,)), ((), ()))
TN = (((0,), (0,)), ((), ()))


def _mm(a, b, dims=NN):
    return lax.dot_general(a.astype(BF16), b.astype(BF16), dims, preferred_element_type=F32)


def _split(x, n):
    if x.dtype == BF16 or n == 1:
        return [x.astype(BF16)]
    parts, r = [], x
    for _ in range(n):
        p = r.astype(BF16)
        parts.append(p)
        r = r - p.astype(F32)
    return parts


def _mm_x(a, b, dims=NN, pa=1, pb=1):
    pa_list, pb_list = _split(a, pa), _split(b, pb)
    order = max(len(pa_list), len(pb_list))
    acc = None
    for i, ai in enumerate(pa_list):
        for j, bj in enumerate(pb_list):
            if i + j < order:
                t = lax.dot_general(ai, bj, dims, preferred_element_type=F32)
                acc = t if acc is None else acc + t
    return acc


def _iota(shape, axis):
    return lax.broadcasted_iota(jnp.int32, shape, axis)


def _tril(n, strict=False):
    r, c = _iota((n, n), 0), _iota((n, n), 1)
    return (c < r) if strict else (c <= r)


def _head_sum(x):
    w = x.shape[-1]
    bd = (_iota((w, w), 0) // HEAD_DIM == _iota((w, w), 1) // HEAD_DIM).astype(BF16)
    return _mm_x(x, bd, pa=3)


def _head_rms(x, g):
    ms = _head_sum(x * x) * (1.0 / HEAD_DIM)
    return x * lax.rsqrt(ms + EPS) * g


def _lo_mask(shape):
    return _iota(shape, len(shape) - 1) % LANES < HEAD_DIM


def _cparams(sem):
    return pltpu.CompilerParams(dimension_semantics=sem, vmem_limit_bytes=VMEM_LIMIT)


def _inproj_kernel(x_ref, g_ref, w_ref, o_ref, *, col_chunk):
    x = x_ref[...]
    ms = jnp.mean(x * x, axis=-1, keepdims=True)
    h = (x * lax.rsqrt(ms + EPS) * g_ref[...]).astype(BF16)
    n = o_ref.shape[1]
    for c0 in range(0, n, col_chunk):
        c1 = min(c0 + col_chunk, n)
        o_ref[:, c0:c1] = jnp.dot(h, w_ref[:, c0:c1], preferred_element_type=F32)


def _inproj(x2, g, w_bf16, tm=256):
    m, d = x2.shape
    n = w_bf16.shape[1]
    return pl.pallas_call(
        functools.partial(_inproj_kernel, col_chunk=512),
        grid=(m // tm,),
        in_specs=[pl.BlockSpec((tm, d), lambda i: (i, 0)),
                  pl.BlockSpec((1, d), lambda i: (0, 0)),
                  pl.BlockSpec((d, n), lambda i: (0, 0))],
        out_specs=pl.BlockSpec((tm, n), lambda i: (i, 0)),
        out_shape=jax.ShapeDtypeStruct((m, n), F32),
        compiler_params=_cparams(("parallel",)),
        name="inproj",
    )(x2, g, w_bf16)


def _outproj_kernel(ya, yb, yc, yd, ym, w_ref, x_ref, o_ref):
    mixed = jnp.concatenate([ya[...], yb[...], yc[...], yd[...], ym[...]], axis=1)
    o_ref[...] = x_ref[...] + jnp.dot(mixed, w_ref[...], preferred_element_type=F32)


def _outproj(ys, w_bf16, x2, tm=512):
    m, d = x2.shape
    yspec = pl.BlockSpec((tm, GROUP_W), lambda i: (i, 0))
    return pl.pallas_call(
        _outproj_kernel,
        grid=(m // tm,),
        in_specs=[yspec] * 5 + [pl.BlockSpec(w_bf16.shape, lambda i: (0, 0)),
                                pl.BlockSpec((tm, d), lambda i: (i, 0))],
        out_specs=pl.BlockSpec((tm, d), lambda i: (i, 0)),
        out_shape=jax.ShapeDtypeStruct((m, d), F32),
        compiler_params=_cparams(("parallel",)),
        name="outproj",
    )(*ys, w_bf16, x2)


def _sgu_kernel(u_ref, v_ref, z_ref, g_ref, w_ref, bias_ref, o_ref):
    t = u_ref.shape[0]
    u = jax.nn.gelu(u_ref[...])
    vn = _head_rms(jax.nn.gelu(v_ref[...]), g_ref[...])
    gate = jax.nn.silu(z_ref[...])
    tril = _tril(CHUNK)
    lo = _lo_mask((CHUNK, LANES))
    ws = [jnp.where(tril, w_ref[g], 0.0).astype(BF16) for g in range(GROUP_HEADS)]
    bias = bias_ref[...]
    for c in range(t // CHUNK):
        sl = slice(c * CHUNK, (c + 1) * CHUNK)
        tiles = []
        for p in range(2):
            vt = vn[sl, p * LANES:(p + 1) * LANES]
            tiles.append(_mm(ws[2 * p], jnp.where(lo, vt, 0.0)) + _mm(ws[2 * p + 1], jnp.where(lo, 0.0, vt)))
        mixed = jnp.concatenate(tiles, axis=1) + bias
        o_ref[sl, :] = (u[sl] * mixed * gate[sl]).astype(o_ref.dtype)


def _sgu(proj, g, w, bias_tile, t=512):
    m = proj.shape[0]
    col = lambda cb: pl.BlockSpec((t, GROUP_W), lambda i, cb=cb: (i, cb))
    full = lambda a: pl.BlockSpec(a.shape, lambda i, n=a.ndim: (0,) * n)
    return pl.pallas_call(
        _sgu_kernel,
        grid=(m // t,),
        in_specs=[col(CB["a_u"]), col(CB["a_v"]), col(CB["a_z"]), full(g), full(w), full(bias_tile)],
        out_specs=pl.BlockSpec((t, GROUP_W), lambda i: (i, 0)),
        out_shape=jax.ShapeDtypeStruct((m, GROUP_W), BF16),
        compiler_params=_cparams(("parallel",)),
        name="sgu",
    )(proj, proj, proj, g, w, bias_tile)


def _memkv_kernel(mem_ref, g_ref, w_ref, kg_ref, k_ref, v_ref):
    x = mem_ref[...]
    ms = jnp.mean(x * x, axis=-1, keepdims=True)
    h = (x * lax.rsqrt(ms + EPS) * g_ref[...]).astype(BF16)
    kv = jnp.dot(h, w_ref[...], preferred_element_type=F32)
    k_ref[...] = _head_rms(kv[:, :GROUP_W], kg_ref[...]).astype(BF16)
    v_ref[...] = kv[:, GROUP_W:].astype(BF16)


def _memkv(mem2, g, w_bf16, kg, n_mem):
    m, d = mem2.shape
    full = lambda a: pl.BlockSpec(a.shape, lambda i, n=a.ndim: (0,) * n)
    out = pl.BlockSpec((n_mem, GROUP_W), lambda i: (i, 0))
    return pl.pallas_call(
        _memkv_kernel,
        grid=(m // n_mem,),
        in_specs=[pl.BlockSpec((n_mem, d), lambda i: (i, 0)), full(g), full(w_bf16), full(kg)],
        out_specs=[out, out],
        out_shape=[jax.ShapeDtypeStruct((m, GROUP_W), BF16)] * 2,
        compiler_params=_cparams(("parallel",)),
        name="memkv",
    )(mem2, g, w_bf16, kg)


def _memattn_kernel(q_ref, z_ref, k_ref, v_ref, qg_ref, o_ref):
    q = (_head_rms(q_ref[...], qg_ref[...]) * SCALE).astype(BF16)
    k = k_ref[...]
    v = v_ref[...]
    t = q.shape[0]
    lo_q = _lo_mask((t, LANES))
    lo_v = _lo_mask((v.shape[0], LANES))
    zero = jnp.zeros((), BF16)
    outs = []
    for p in range(2):
        sl = slice(p * LANES, (p + 1) * LANES)
        qp, kp, vp = q[:, sl], k[:, sl], v[:, sl]
        acc = None
        for hh in range(2):
            mq = lo_q if hh == 0 else ~lo_q
            mv = lo_v if hh == 0 else ~lo_v
            s = _mm(jnp.where(mq, qp, zero), kp, NT)
            e = jnp.exp(s - jnp.max(s, axis=-1, keepdims=True))
            pr = e / jnp.sum(e, axis=-1, keepdims=True)
            pv = _mm(pr, jnp.where(mv, vp, zero))
            acc = pv if acc is None else acc + pv
        outs.append(acc)
    y = jnp.concatenate(outs, axis=1)
    o_ref[...] = (y * jax.nn.silu(z_ref[...])).astype(o_ref.dtype)


def _memattn(proj, kn, vb, qg, bsz, n_mem, t=512):
    m = proj.shape[0]
    nt = m // bsz // t
    col = lambda cb: pl.BlockSpec((t, GROUP_W), lambda b, i, cb=cb: (b * nt + i, cb))
    kv = pl.BlockSpec((n_mem, GROUP_W), lambda b, i: (b, 0))
    return pl.pallas_call(
        _memattn_kernel,
        grid=(bsz, nt),
        in_specs=[col(CB["m_q"]), col(CB["m_z"]), kv, kv, pl.BlockSpec(qg.shape, lambda b, i: (0, 0))],
        out_specs=pl.BlockSpec((t, GROUP_W), lambda b, i: (b * nt + i, 0)),
        out_shape=jax.ShapeDtypeStruct((m, GROUP_W), BF16),
        compiler_params=_cparams(("parallel", "parallel")),
        name="memattn",
    )(proj, proj, kn, vb, qg)


def _foxprep_kernel(q_ref, k_ref, v_ref, sm_ref, qg_ref, kg_ref, fb_ref,
                    qn_ref, kn_ref, vb_ref, fc_ref, fr_ref, carry_ref):
    @pl.when(pl.program_id(1) == 0)
    def _():
        carry_ref[...] = jnp.zeros_like(carry_ref)

    t = q_ref.shape[0]
    qn_ref[...] = (_head_rms(q_ref[...], qg_ref[...]) * SCALE).astype(BF16)
    kn_ref[...] = _head_rms(k_ref[...], kg_ref[...]).astype(BF16)
    vb_ref[...] = v_ref[...].astype(BF16)
    lf = jax.nn.log_sigmoid(sm_ref[...] + fb_ref[...])
    cs = _mm_x(_tril(t).astype(BF16), lf, pb=3) + carry_ref[0:1, :]
    carry_ref[...] = jnp.broadcast_to(cs[t - 1:t, :], carry_ref.shape)
    fc_ref[...] = cs
    fr_ref[0] = cs.T[0:8, :]


def _foxprep(proj, qg, kg, fb_row, bsz, t=512):
    m = proj.shape[0]
    seq = m // bsz
    nt = seq // t
    col = lambda cb: pl.BlockSpec((t, GROUP_W), lambda b, i, cb=cb: (b * nt + i, cb))
    row = lambda a: pl.BlockSpec(a.shape, lambda b, i: (0, 0))
    wide = pl.BlockSpec((t, GROUP_W), lambda b, i: (b * nt + i, 0))
    return pl.pallas_call(
        _foxprep_kernel,
        grid=(bsz, nt),
        in_specs=[col(CB["b_q"]), col(CB["b_k"]), col(CB["b_v"]),
                  pl.BlockSpec((t, LANES), lambda b, i: (b * nt + i, SMALL_CB)),
                  row(qg), row(kg), row(fb_row)],
        out_specs=[wide, wide, wide,
                   pl.BlockSpec((t, LANES), lambda b, i: (b * nt + i, 0)),
                   pl.BlockSpec((1, 8, t), lambda b, i: (b, 0, i))],
        out_shape=[jax.ShapeDtypeStruct((m, GROUP_W), BF16)] * 3
        + [jax.ShapeDtypeStruct((m, LANES), F32), jax.ShapeDtypeStruct((bsz, 8, seq), F32)],
        scratch_shapes=[pltpu.VMEM((8, LANES), F32)],
        compiler_params=_cparams(("parallel", "arbitrary")),
        name="foxprep",
    )(proj, proj, proj, proj, qg, kg, fb_row)


def _fox_kernel(q_ref, k_ref, v_ref, fc_ref, fr_ref, z_ref, o_ref, m_scr, l_scr, acc_scr, *, tq, tk):
    pair = pl.program_id(1)
    i = pl.program_id(2)
    j = pl.program_id(3)

    @pl.when(j == 0)
    def _():
        m_scr[...] = jnp.full_like(m_scr, -jnp.inf)
        l_scr[...] = jnp.zeros_like(l_scr)
        acc_scr[...] = jnp.zeros_like(acc_scr)

    @pl.when(j * tk <= i * tq + tq - 1)
    def _():
        q = q_ref[...]
        k = k_ref[...]
        v = v_ref[...]
        fc = fc_ref[...]
        fr = fr_ref[0]
        lo_q = _lo_mask((tq, LANES))
        lo_v = _lo_mask((tk, LANES))
        zero = jnp.zeros((), BF16)
        causal = (j * tk + _iota((tq, tk), 1)) <= (i * tq + _iota((tq, tk), 0))
        alphas, pvs = [], []
        for hh in range(2):
            mq = lo_q if hh == 0 else ~lo_q
            mv = lo_v if hh == 0 else ~lo_v
            f_t = jnp.where(pair == 0, fc[:, hh:hh + 1], fc[:, 2 + hh:3 + hh])
            f_s = jnp.where(pair == 0, fr[hh:hh + 1, :], fr[2 + hh:3 + hh, :])
            s = _mm(jnp.where(mq, q, zero), k, NT) + (f_t - f_s)
            s = jnp.where(causal, s, -jnp.inf)
            m_old = m_scr[hh][:, 0:1]
            m_new = jnp.maximum(m_old, jnp.max(s, axis=-1, keepdims=True))
            alpha = jnp.exp(m_old - m_new)
            pe = jnp.exp(s - m_new)
            l_new = alpha * l_scr[hh][:, 0:1] + jnp.sum(pe, axis=-1, keepdims=True)
            m_scr[hh] = jnp.broadcast_to(m_new, (tq, LANES))
            l_scr[hh] = jnp.broadcast_to(l_new, (tq, LANES))
            alphas.append(alpha)
            pvs.append(_mm(pe, jnp.where(mv, v, zero)))
        alpha_pair = jnp.where(lo_q, alphas[0], alphas[1])
        acc_scr[...] = acc_scr[...] * alpha_pair + pvs[0] + pvs[1]

    @pl.when(j == pl.num_programs(3) - 1)
    def _():
        lo_q = _lo_mask((tq, LANES))
        l_pair = jnp.where(lo_q, l_scr[0], l_scr[1])
        y = acc_scr[...] / l_pair
        o_ref[...] = (y * jax.nn.silu(z_ref[...])).astype(o_ref.dtype)


def _fox(proj, qn, kn, vb, fc, fr, bsz, tq=256, tk=256):
    m = proj.shape[0]
    seq = m // bsz
    nq, nk = seq // tq, seq // tk

    def kv_map(b, p, i, j):
        jmax = (i * tq + tq - 1) // tk
        return (b * nk + jnp.minimum(j, jmax), p)

    def fr_map(b, p, i, j):
        jmax = (i * tq + tq - 1) // tk
        return (b, 0, jnp.minimum(j, jmax))

    return pl.pallas_call(
        functools.partial(_fox_kernel, tq=tq, tk=tk),
        grid=(bsz, 2, nq, nk),
        in_specs=[pl.BlockSpec((tq, LANES), lambda b, p, i, j: (b * nq + i, p)),
                  pl.BlockSpec((tk, LANES), kv_map),
                  pl.BlockSpec((tk, LANES), kv_map),
                  pl.BlockSpec((tq, LANES), lambda b, p, i, j: (b * nq + i, 0)),
                  pl.BlockSpec((1, 8, tk), fr_map),
                  pl.BlockSpec((tq, LANES), lambda b, p, i, j: (b * nq + i, 2 * CB["b_z"] + p))],
        out_specs=pl.BlockSpec((tq, LANES), lambda b, p, i, j: (b * nq + i, p)),
        out_shape=jax.ShapeDtypeStruct((m, GROUP_W), BF16),
        scratch_shapes=[pltpu.VMEM((2, tq, LANES), F32), pltpu.VMEM((2, tq, LANES), F32),
                        pltpu.VMEM((tq, LANES), F32)],
        compiler_params=_cparams(("parallel", "parallel", "parallel", "arbitrary")),
        name="fox",
    )(qn, kn, vb, fc, fr, proj)


def _mlstm_kernel(q_ref, k_ref, v_ref, og_ref, z_ref, sm_ref, cw_ref, cb_ref, ib_ref, fb_ref, g_ref,
                  out_ref, xbuf, c_scr, n_scr, m_scr):
    t = CHUNK

    @pl.when(pl.program_id(1) == 0)
    def _():
        xbuf[0:8, :] = jnp.zeros((8, 2 * GROUP_W), F32)
        c_scr[...] = jnp.zeros_like(c_scr)
        n_scr[...] = jnp.zeros_like(n_scr)
        m_scr[...] = jnp.zeros_like(m_scr)

    xbuf[8:8 + t, 0:GROUP_W] = q_ref[...]
    xbuf[8:8 + t, GROUP_W:] = k_ref[...]
    conv = cb_ref[...]
    for jj in range(CONV_K):
        conv = conv + cw_ref[jj:jj + 1, :] * xbuf[8 - (CONV_K - 1) + jj:8 - (CONV_K - 1) + jj + t, :]
    xbuf[0:8, :] = xbuf[t:t + 8, :]
    qk = jax.nn.silu(conv)
    q = qk[:, :GROUP_W]
    k = qk[:, GROUP_W:] * SCALE
    v = v_ref[...]

    sm = sm_ref[...]
    li = sm + ib_ref[...]
    lf = jax.nn.log_sigmoid(sm + fb_ref[...])
    bcum = _mm_x(_tril(t).astype(BF16), lf, pb=3)
    b_t = bcum.T
    i_t = li.T

    tril = _tril(t)
    lo = _lo_mask((t, LANES))
    lo_row = _lo_mask((1, LANES))
    blockdiag = (_iota((LANES, LANES), 0) < HEAD_DIM) == (_iota((LANES, LANES), 1) < HEAD_DIM)
    lane_row = _iota((1, LANES), 1)
    m_row = m_scr[0:1, :]
    m_row_new = m_row
    h_tiles = []
    for p in range(2):
        sl = slice(p * LANES, (p + 1) * LANES)
        qp, kp, vp = q[:, sl], k[:, sl], v[:, sl]
        c_p = c_scr[p]
        n_p = n_scr[p][0:1, :]
        q_c = _mm(qp, c_p)
        q_n = qp * n_p
        h_tile = None
        w_cols, cds = [], []
        for hh in range(2):
            h = 2 * p + hh
            mask = lo if hh == 0 else ~lo
            b_c = bcum[:, SM_CF + h:SM_CF + h + 1]
            b_r = b_t[SM_CF + h:SM_CF + h + 1, :]
            i_c = li[:, SM_CI + h:SM_CI + h + 1]
            i_r = i_t[SM_CI + h:SM_CI + h + 1, :]
            m_st = m_row[:, h:h + 1]
            log_d = jnp.where(tril, b_c - b_r + i_r, -jnp.inf)
            inter = b_c + m_st
            m_t = jnp.maximum(jnp.max(log_d, axis=-1, keepdims=True), inter)
            s = _mm(jnp.where(mask, qp, 0.0), kp, NT) * jnp.exp(log_d - m_t)
            w_inter = jnp.exp(inter - m_t)
            num = _mm(s, jnp.where(mask, vp, 0.0)) + w_inter * jnp.where(mask, q_c, 0.0)
            den = (jnp.sum(s, axis=-1, keepdims=True)
                   + w_inter * jnp.sum(jnp.where(mask, q_n, 0.0), axis=-1, keepdims=True))
            hv = num / jnp.maximum(jnp.abs(den), jnp.exp(-m_t))
            h_tile = hv if h_tile is None else h_tile + hv
            g = b_c[t - 1:t, :]
            m_new = jnp.maximum(g + m_st, jnp.max(g - b_r + i_r, axis=-1, keepdims=True))
            w_cols.append(jnp.exp(g - b_c + i_c - m_new))
            cds.append(jnp.exp(g + m_st - m_new))
            m_row_new = jnp.where(lane_row == h, m_new, m_row_new)
        h_tiles.append(h_tile)
        kw = kp * jnp.where(lo, w_cols[0], w_cols[1])
        cd_row = jnp.where(lo_row, cds[0], cds[1])
        c_scr[p] = cd_row * c_p + jnp.where(blockdiag, _mm(kw, vp, TN), 0.0)
        n_new = cd_row * n_p + jnp.sum(kw, axis=0, keepdims=True)
        n_scr[p] = jnp.broadcast_to(n_new, (8, LANES))
    m_scr[...] = jnp.broadcast_to(m_row_new, m_scr.shape)

    hcat = jax.nn.sigmoid(og_ref[...]) * jnp.concatenate(h_tiles, axis=1)
    y = _head_rms(hcat, g_ref[...])
    out_ref[...] = (y * jax.nn.silu(z_ref[...])).astype(out_ref.dtype)


def _mlstm(proj, cw, cb, ib_row, fb_row, g, bsz):
    m = proj.shape[0]
    nc = m // bsz // CHUNK
    col = lambda cb_: pl.BlockSpec((CHUNK, GROUP_W), lambda b, c, cb_=cb_: (b * nc + c, cb_))
    full = lambda a: pl.BlockSpec(a.shape, lambda b, c, n=a.ndim: (0,) * n)
    return pl.pallas_call(
        _mlstm_kernel,
        grid=(bsz, nc),
        in_specs=[col(CB["c_q"]), col(CB["c_k"]), col(CB["c_v"]), col(CB["c_o"]), col(CB["c_z"]),
                  pl.BlockSpec((CHUNK, LANES), lambda b, c: (b * nc + c, SMALL_CB)),
                  full(cw), full(cb), full(ib_row), full(fb_row), full(g)],
        out_specs=pl.BlockSpec((CHUNK, GROUP_W), lambda b, c: (b * nc + c, 0)),
        out_shape=jax.ShapeDtypeStruct((m, GROUP_W), BF16),
        scratch_shapes=[pltpu.VMEM((CHUNK + 8, 2 * GROUP_W), F32), pltpu.VMEM((2, LANES, LANES), F32),
                        pltpu.VMEM((2, 8, LANES), F32), pltpu.VMEM((8, LANES), F32)],
        compiler_params=_cparams(("parallel", "arbitrary")),
        name="mlstm",
    )(proj, proj, proj, proj, proj, proj, cw, cb, ib_row, fb_row, g)


PASSES_AA = 2
PASSES_INV = 2
PASSES_STATE = 2


def _mmp(a, b, dims=NN, passes=1):
    return _mm_x(a, b, dims, pa=passes, pb=passes)


def _tri_inverse(a):
    n = a.shape[0]
    r, c = _iota((n, n), 0), _iota((n, n), 1)
    same = lambda b: (r // b) == (c // b)
    a8 = jnp.where(same(8), a, 0.0)
    t = jnp.where(r == c, 1.0, 0.0) + a8
    p = _mmp(a8, a8, passes=PASSES_INV)
    t = t + _mmp(t, p, passes=PASSES_INV)
    p = _mmp(p, p, passes=PASSES_INV)
    t = t + _mmp(t, p, passes=PASSES_INV)
    b = 8
    while b < n:
        a_off = jnp.where(same(2 * b) & ~same(b), a, 0.0)
        t = t + _mmp(t, _mmp(a_off, t, passes=PASSES_INV), passes=PASSES_INV)
        b *= 2
    return t


def _rwkv_prep_kernel(r_ref, k_ref, v_ref, z_ref, sm_ref, pr_ref, pk_ref, pv_ref, pz_ref, psm_ref,
                      mur_ref, muk_ref, muv_ref, muz_ref, mus_ref,
                      w0_ref, w2_ref, a0_ref, a2_ref, kk_ref, ka_ref, rk_ref,
                      als_ref, rs_ref, bh_ref, tg_ref, uind_ref, yind_ref, bonus_ref, gate_ref,
                      sconst_ref, wc_ref, *, nc):
    t = CHUNK
    first = (pl.program_id(0) % nc) == 0
    row0 = _iota((t, 1), 0) == 0

    def shifted(x_ref, p_ref, mu_ref):
        x = x_ref[...]
        prev_row = jnp.where(first, 0.0, p_ref[7:8, :])
        prev = jnp.where(row0, prev_row, pltpu.roll(x, 1, axis=0))
        return x + mu_ref[...] * (prev - x)

    r = shifted(r_ref, pr_ref, mur_ref)
    k = shifted(k_ref, pk_ref, muk_ref)
    v = shifted(v_ref, pv_ref, muv_ref)
    z = shifted(z_ref, pz_ref, muz_ref)
    sm = shifted(sm_ref, psm_ref, mus_ref)

    w_log = -jax.nn.softplus(-(w0_ref[...] + _mm(jnp.tanh(sm), w2_ref[...]))) - 0.5
    ld = -jnp.exp(w_log)
    a = jax.nn.sigmoid(a0_ref[...] + _mm(sm, a2_ref[...]))
    kk = k * kk_ref[...]
    kk = kk / jnp.maximum(jnp.sqrt(_head_sum(kk * kk)), 1e-12)
    k2 = k * (1.0 + (a - 1.0) * ka_ref[...])
    kka = kk * a

    lw = _mm_x(_tril(t).astype(BF16), ld, pb=3)
    lw_ex = lw - ld
    lw_mid = lw[t // 2 - 1:t // 2, :]
    lw_end = lw[t - 1:t, :]
    e_in = jnp.exp(lw - lw_mid)
    e_out = jnp.exp(lw_mid - lw)
    al_m = -kk * jnp.exp(lw_ex - lw_mid)
    r_m = r * e_in
    be_m = kka * e_out
    k_m = k2 * e_out
    al_s = -kk * jnp.exp(lw_ex)
    r_s = r * jnp.exp(lw)
    e_end = jnp.exp(lw_end - lw)
    b_h = kka * e_end
    k_h = k2 * e_end

    als_ref[...] = al_s.astype(BF16)
    rs_ref[...] = r_s.astype(BF16)
    bh_ref[...] = b_h.astype(BF16)
    bonus_ref[...] = _head_sum(r * k2 * rk_ref[...]) * v
    gate_ref[...] = jax.nn.silu(z)
    wc_ref[0] = jnp.broadcast_to(jnp.exp(lw_end), (8, GROUP_W))

    strict = _tril(t, strict=True)
    incl = _tril(t)
    lo = _lo_mask((t, LANES))
    for p in range(2):
        sl = slice(p * LANES, (p + 1) * LANES)
        rhs = jnp.concatenate([be_m[:, sl], k_m[:, sl]], axis=0)
        u_pair, y_pair, s_pair = None, None, None
        for hh in range(2):
            h = 2 * p + hh
            mask = lo if hh == 0 else ~lo
            lhs = jnp.concatenate([jnp.where(mask, al_m[:, sl], 0.0), jnp.where(mask, r_m[:, sl], 0.0)], axis=0)
            aa = _mmp(lhs, rhs, NT, passes=PASSES_AA)
            a_ab = jnp.where(strict, aa[:t, :t], 0.0)
            a_ak = jnp.where(strict, aa[:t, t:], 0.0)
            a_rb = jnp.where(incl, aa[t:, :t], 0.0)
            a_rk = jnp.where(incl, aa[t:, t:], 0.0)
            tinv = _tri_inverse(a_ab)
            v_h = jnp.where(mask, v[:, sl], 0.0)
            u_ind = _mmp(tinv, _mmp(a_ak, v_h, passes=PASSES_INV), passes=PASSES_INV)
            g_mat = _mmp(a_rb, tinv, passes=PASSES_INV)
            y_ind = _mmp(a_rb, u_ind, passes=PASSES_INV) + _mmp(a_rk, v_h, passes=PASSES_INV)
            s_c = (_mmp(u_ind, jnp.where(mask, b_h[:, sl], 0.0), TN, passes=PASSES_INV)
                   + _mmp(v_h, jnp.where(mask, k_h[:, sl], 0.0), TN, passes=PASSES_INV))
            tg_ref[:, h * 2 * t:h * 2 * t + t] = tinv.astype(BF16)
            tg_ref[:, h * 2 * t + t:(h + 1) * 2 * t] = g_mat.astype(BF16)
            u_pair = u_ind if u_pair is None else u_pair + u_ind
            y_pair = y_ind if y_pair is None else y_pair + y_ind
            s_pair = s_c if s_pair is None else s_pair + s_c
        uind_ref[:, sl] = u_pair
        yind_ref[:, sl] = y_pair
        sconst_ref[:, sl] = s_pair


def _rwkv_prep(proj, prm, bsz):
    m = proj.shape[0]
    nc = m // bsz // CHUNK
    sub = CHUNK // 8
    col = lambda cb: pl.BlockSpec((CHUNK, GROUP_W), lambda i, cb=cb: (i, cb))
    prev = lambda cb: pl.BlockSpec((8, GROUP_W), lambda i, cb=cb: (jnp.maximum(i * sub - 1, 0), cb))
    full = lambda a: pl.BlockSpec(a.shape, lambda i, n=a.ndim: (0,) * n)
    wide = pl.BlockSpec((CHUNK, GROUP_W), lambda i: (i, 0))
    params = [prm["mu_r"], prm["mu_k"], prm["mu_v"], prm["mu_z"], prm["mu_s"], prm["w0"], prm["w2p"],
              prm["a0"], prm["a2p"], prm["k_k"], prm["k_a"], prm["r_k"]]
    wide_bf = jax.ShapeDtypeStruct((m, GROUP_W), BF16)
    wide_f = jax.ShapeDtypeStruct((m, GROUP_W), F32)
    return pl.pallas_call(
        functools.partial(_rwkv_prep_kernel, nc=nc),
        grid=(m // CHUNK,),
        in_specs=[col(CB["d_r"]), col(CB["d_k"]), col(CB["d_v"]), col(CB["d_z"]),
                  pl.BlockSpec((CHUNK, LANES), lambda i: (i, SMALL_CB)),
                  prev(CB["d_r"]), prev(CB["d_k"]), prev(CB["d_v"]), prev(CB["d_z"]),
                  pl.BlockSpec((8, LANES), lambda i: (jnp.maximum(i * sub - 1, 0), SMALL_CB))]
        + [full(a) for a in params],
        out_specs=[wide, wide, wide,
                   pl.BlockSpec((CHUNK, 4 * GROUP_W), lambda i: (i, 0)),
                   wide, wide, wide, wide, wide,
                   pl.BlockSpec((1, 8, GROUP_W), lambda i: (i, 0, 0))],
        out_shape=[wide_bf, wide_bf, wide_bf, jax.ShapeDtypeStruct((m, 4 * GROUP_W), BF16),
                   wide_f, wide_f, wide_f, wide_f, wide_f,
                   jax.ShapeDtypeStruct((m // CHUNK, 8, GROUP_W), F32)],
        compiler_params=_cparams(("parallel",)),
        name="rwkv_prep",
    )(proj, proj, proj, proj, proj, proj, proj, proj, proj, proj, *params)


def _rwkv_scan_kernel(als_ref, rs_ref, bh_ref, tg_ref, yind_ref, bonus_ref, gate_ref, sconst_ref, wc_ref,
                      g_ref, out_ref, s_scr):
    t = CHUNK

    @pl.when(pl.program_id(1) == 0)
    def _():
        s_scr[...] = jnp.zeros_like(s_scr)

    lo = _lo_mask((t, LANES))
    blockdiag = (_iota((LANES, LANES), 0) < HEAD_DIM) == (_iota((LANES, LANES), 1) < HEAD_DIM)
    zero = jnp.zeros((), F32)
    y_tiles = []
    for p in range(2):
        sl = slice(p * LANES, (p + 1) * LANES)
        s_p = s_scr[p]
        lhs = jnp.concatenate([als_ref[:, sl], rs_ref[:, sl]], axis=0)
        x = _mm_x(lhs, s_p, NT, pb=PASSES_STATE)
        x0, rs0 = x[:t], x[t:]
        tx, gx = None, None
        for hh in range(2):
            h = 2 * p + hh
            mask = lo if hh == 0 else ~lo
            tg = jnp.concatenate([tg_ref[:, h * 2 * t:h * 2 * t + t],
                                  tg_ref[:, h * 2 * t + t:(h + 1) * 2 * t]], axis=0)
            res = _mm_x(tg, jnp.where(mask, x0, zero), pb=PASSES_STATE)
            tx = res[:t] if tx is None else tx + res[:t]
            gx = res[t:] if gx is None else gx + res[t:]
        y_tiles.append(rs0 + gx + yind_ref[:, sl])
        upd = _mm_x(tx, bh_ref[:, sl], TN, pa=PASSES_STATE)
        s_scr[p] = s_p * wc_ref[0][0:1, sl] + sconst_ref[:, sl] + jnp.where(blockdiag, upd, zero)
    y = _head_rms(jnp.concatenate(y_tiles, axis=1), g_ref[...]) + bonus_ref[...]
    out_ref[...] = (y * gate_ref[...]).astype(out_ref.dtype)


def _rwkv_scan(prep, g, bsz):
    als, rs, bh, tg, _uind, yind, bonus, gate, sconst, wc = prep
    m = als.shape[0]
    nc = m // bsz // CHUNK
    wide = pl.BlockSpec((CHUNK, GROUP_W), lambda b, c: (b * nc + c, 0))
    return pl.pallas_call(
        _rwkv_scan_kernel,
        grid=(bsz, nc),
        in_specs=[wide, wide, wide,
                  pl.BlockSpec((CHUNK, 4 * GROUP_W), lambda b, c: (b * nc + c, 0)),
                  wide, wide, wide, wide,
                  pl.BlockSpec((1, 8, GROUP_W), lambda b, c: (b * nc + c, 0, 0)),
                  pl.BlockSpec(g.shape, lambda b, c: (0, 0))],
        out_specs=wide,
        out_shape=jax.ShapeDtypeStruct((m, GROUP_W), BF16),
        scratch_shapes=[pltpu.VMEM((2, LANES, LANES), F32)],
        compiler_params=_cparams(("parallel", "arbitrary")),
        name="rwkv_scan",
    )(als, rs, bh, tg, yind, bonus, gate, sconst, wc, g)


def _row(v, width=None, offset=0):
    v = v.astype(F32).reshape(-1)
    width = v.shape[0] if width is None else width
    return jnp.zeros((1, width), F32).at[0, offset:offset + v.shape[0]].set(v)


def _layout_w_in(w_in):
    gw = GROUP_W
    a0 = 0
    b0 = a0 + 3 * gw
    c0 = b0 + 4 * gw + GROUP_HEADS
    d0 = c0 + 5 * gw + 2 * GROUP_HEADS
    m0 = d0 + 4 * gw + DECAY_RANK + A_RANK
    c = lambda s, w: w_in[:, s:s + w]
    wide = [c(a0, gw), c(a0 + gw, gw), c(a0 + 2 * gw, gw),
            c(b0, gw), c(b0 + gw, gw), c(b0 + 2 * gw, gw), c(b0 + 3 * gw + GROUP_HEADS, gw),
            c(c0, gw), c(c0 + gw, gw), c(c0 + 2 * gw, gw), c(c0 + 3 * gw + 2 * GROUP_HEADS, gw),
            c(c0 + 4 * gw + 2 * GROUP_HEADS, gw),
            c(d0, gw), c(d0 + gw + DECAY_RANK, gw), c(d0 + 2 * gw + DECAY_RANK, gw),
            c(d0 + 3 * gw + DECAY_RANK + A_RANK, gw),
            c(m0, gw), c(m0 + gw, gw)]
    small = jnp.zeros((w_in.shape[0], LANES), w_in.dtype)
    small = small.at[:, SM_BF:SM_BF + GROUP_HEADS].set(c(b0 + 3 * gw, GROUP_HEADS))
    small = small.at[:, SM_CI:SM_CI + GROUP_HEADS].set(c(c0 + 3 * gw, GROUP_HEADS))
    small = small.at[:, SM_CF:SM_CF + GROUP_HEADS].set(c(c0 + 3 * gw + GROUP_HEADS, GROUP_HEADS))
    small = small.at[:, SM_WLO:SM_WLO + DECAY_RANK].set(c(d0 + gw, DECAY_RANK))
    small = small.at[:, SM_ALO:SM_ALO + A_RANK].set(c(d0 + 3 * gw + DECAY_RANK, A_RANK))
    return jnp.concatenate(wide + [small], axis=1).astype(BF16)


def _layout_mu(mu):
    gw = GROUP_W
    o_w, o_k, o_v, o_a, o_z = gw, gw + DECAY_RANK, 2 * gw + DECAY_RANK, 3 * gw + DECAY_RANK, 3 * gw + DECAY_RANK + A_RANK
    mu_s = jnp.zeros((1, LANES), F32)
    mu_s = mu_s.at[0, SM_WLO:SM_WLO + DECAY_RANK].set(mu[o_w:o_w + DECAY_RANK])
    mu_s = mu_s.at[0, SM_ALO:SM_ALO + A_RANK].set(mu[o_a:o_a + A_RANK])
    return dict(mu_r=_row(mu[0:gw]), mu_k=_row(mu[o_k:o_k + gw]), mu_v=_row(mu[o_v:o_v + gw]),
                mu_z=_row(mu[o_z:o_z + gw]), mu_s=mu_s)


def _pad_rows(w, offset):
    return jnp.zeros((LANES, w.shape[1]), F32).at[offset:offset + w.shape[0]].set(w).astype(BF16)


def _layer(x2, mem2, bsz, n_mem, norm_g, w_in, w_out, sgu_norm_g, sgu_w, sgu_b, fox_q_g, fox_k_g, fox_f_b,
           mlstm_conv_w, mlstm_conv_b, mlstm_i_b, mlstm_f_b, mlstm_out_g,
           rwkv_mu, rwkv_w0, rwkv_w2, rwkv_a0, rwkv_a2, rwkv_k_k, rwkv_k_a, rwkv_r_k, rwkv_ln_g,
           mem_norm_g, mem_w_kv, mem_q_g, mem_k_g):
    tile_h = lambda g: _row(jnp.tile(g, GROUP_HEADS))
    proj = _inproj(x2, _row(norm_g), _layout_w_in(w_in))

    ya = _sgu(proj, _row(sgu_norm_g), sgu_w, jnp.repeat(sgu_b.T, HEAD_DIM, axis=1))

    qn, kn, vb, fc, fr = _foxprep(proj, tile_h(fox_q_g), tile_h(fox_k_g), _row(fox_f_b, LANES, SM_BF), bsz)
    yb = _fox(proj, qn, kn, vb, fc, fr, bsz)

    cw = jnp.zeros((8, 2 * GROUP_W), F32).at[:CONV_K].set(mlstm_conv_w)
    yc = _mlstm(proj, cw, _row(mlstm_conv_b), _row(mlstm_i_b, LANES, SM_CI), _row(mlstm_f_b, LANES, SM_CF),
                _row(mlstm_out_g), bsz)

    prm = _layout_mu(rwkv_mu)
    prm.update(w0=_row(rwkv_w0), w2p=_pad_rows(rwkv_w2, SM_WLO), a0=_row(rwkv_a0), a2p=_pad_rows(rwkv_a2, SM_ALO),
               k_k=_row(rwkv_k_k), k_a=_row(rwkv_k_a), r_k=_row(rwkv_r_k))
    yd = _rwkv_scan(_rwkv_prep(proj, prm, bsz), _row(rwkv_ln_g), bsz)

    mk, mv = _memkv(mem2, _row(mem_norm_g), mem_w_kv.astype(BF16), tile_h(mem_k_g), n_mem)
    ym = _memattn(proj, mk, mv, tile_h(mem_q_g), bsz, n_mem)

    return _outproj([ya, yb, yc, yd, ym], w_out.astype(BF16), x2)


def kernel(x, mem, norm_g, w_in, w_out, sgu_norm_g, sgu_w, sgu_b, fox_q_g, fox_k_g, fox_f_b, mlstm_conv_w, mlstm_conv_b, mlstm_i_b, mlstm_f_b, mlstm_out_g, rwkv_mu, rwkv_w0, rwkv_w2, rwkv_a0, rwkv_a2, rwkv_k_k, rwkv_k_a, rwkv_r_k, rwkv_ln_g, mem_norm_g, mem_w_kv, mem_q_g, mem_k_g):
    bsz, seq, d = x.shape
    n_mem = mem.shape[1]
    stacked = (norm_g, w_in, w_out, sgu_norm_g, sgu_w, sgu_b, fox_q_g, fox_k_g, fox_f_b,
               mlstm_conv_w, mlstm_conv_b, mlstm_i_b, mlstm_f_b, mlstm_out_g,
               rwkv_mu, rwkv_w0, rwkv_w2, rwkv_a0, rwkv_a2, rwkv_k_k, rwkv_k_a, rwkv_r_k, rwkv_ln_g,
               mem_norm_g, mem_w_kv, mem_q_g, mem_k_g)
    x2 = x.reshape(bsz * seq, d)
    mem2 = mem.reshape(bsz * n_mem, d)
    for layer in range(norm_g.shape[0]):
        x2 = _layer(x2, mem2, bsz, n_mem, *[p[layer] for p in stacked])
    return x2.reshape(bsz, seq, d)
```

```python
import functools

import jax
import jax.numpy as jnp
from jax import lax
from jax.experimental import pallas as pl
from jax.experimental.pallas import tpu as pltpu

F32 = jnp.float32
BF16 = jnp.bfloat16

HEAD_DIM = 64
GROUP_HEADS = 4
GROUP_W = GROUP_HEADS * HEAD_DIM
LANES = 128
CHUNK = 128
CONV_K = 4
EPS = 1e-6
SCALE = HEAD_DIM ** -0.5

CB = dict(a_u=0, a_v=1, a_z=2, b_q=3, b_k=4, b_v=5, b_z=6, c_q=7, c_k=8, c_v=9, c_o=10, c_z=11,
          d_r=12, d_k=13, d_v=14, d_z=15, m_q=16, m_z=17)
N_WIDE = 18 * GROUP_W
SMALL_CB = N_WIDE // LANES
NP = N_WIDE + LANES
SM_BF, SM_CI, SM_CF, SM_WLO, SM_ALO = 0, 4, 8, 16, 32
DECAY_RANK = 16
A_RANK = 16

VMEM_LIMIT = 48 * 1024 * 1024

NN = (((1,), (0,)), ((), ()))
NT = (((1,), (1,)), ((), ()))
TN = (((0,), (0,)), ((), ()))


def _mm(a, b, dims=NN):
    return lax.dot_general(a.astype(BF16), b.astype(BF16), dims, preferred_element_type=F32)


def _split(x, n):
    if x.dtype == BF16 or n == 1:
        return [x.astype(BF16)]
    parts, r = [], x
    for _ in range(n):
        p = r.astype(BF16)
        parts.append(p)
        r = r - p.astype(F32)
    return parts


def _mm_x(a, b, dims=NN, pa=1, pb=1):
    pa_list, pb_list = _split(a, pa), _split(b, pb)
    order = max(len(pa_list), len(pb_list))
    acc = None
    for i, ai in enumerate(pa_list):
        for j, bj in enumerate(pb_list):
            if i + j < order:
                t = lax.dot_general(ai, bj, dims, preferred_element_type=F32)
                acc = t if acc is None else acc + t
    return acc


def _iota(shape, axis):
    return lax.broadcasted_iota(jnp.int32, shape, axis)


def _tril(n, strict=False):
    r, c = _iota((n, n), 0), _iota((n, n), 1)
    return (c < r) if strict else (c <= r)


def _head_sum(x):
    w = x.shape[-1]
    bd = (_iota((w, w), 0) // HEAD_DIM == _iota((w, w), 1) // HEAD_DIM).astype(BF16)
    return _mm_x(x, bd, pa=3)


def _head_rms(x, g):
    ms = _head_sum(x * x) * (1.0 / HEAD_DIM)
    return x * lax.rsqrt(ms + EPS) * g


def _lo_mask(shape):
    return _iota(shape, len(shape) - 1) % LANES < HEAD_DIM


def _cparams(sem):
    return pltpu.CompilerParams(dimension_semantics=sem, vmem_limit_bytes=VMEM_LIMIT)


def _inproj_kernel(x_ref, g_ref, w_ref, o_ref, *, col_chunk):
    x = x_ref[...]
    ms = jnp.mean(x * x, axis=-1, keepdims=True)
    h = (x * lax.rsqrt(ms + EPS) * g_ref[...]).astype(BF16)
    n = o_ref.shape[1]
    for c0 in range(0, n, col_chunk):
        c1 = min(c0 + col_chunk, n)
        o_ref[:, c0:c1] = jnp.dot(h, w_ref[:, c0:c1], preferred_element_type=F32)


def _inproj(x2, g, w_bf16, tm=256):
    m, d = x2.shape
    n = w_bf16.shape[1]
    return pl.pallas_call(
        functools.partial(_inproj_kernel, col_chunk=512),
        grid=(m // tm,),
        in_specs=[pl.BlockSpec((tm, d), lambda i: (i, 0)),
                  pl.BlockSpec((1, d), lambda i: (0, 0)),
                  pl.BlockSpec((d, n), lambda i: (0, 0))],
        out_specs=pl.BlockSpec((tm, n), lambda i: (i, 0)),
        out_shape=jax.ShapeDtypeStruct((m, n), F32),
        compiler_params=_cparams(("parallel",)),
        name="inproj",
    )(x2, g, w_bf16)


def _outproj_kernel(ya, yb, yc, yd, ym, w_ref, x_ref, o_ref):
    mixed = jnp.concatenate([ya[...], yb[...], yc[...], yd[...], ym[...]], axis=1)
    o_ref[...] = x_ref[...] + jnp.dot(mixed, w_ref[...], preferred_element_type=F32)


def _outproj(ys, w_bf16, x2, tm=512):
    m, d = x2.shape
    yspec = pl.BlockSpec((tm, GROUP_W), lambda i: (i, 0))
    return pl.pallas_call(
        _outproj_kernel,
        grid=(m // tm,),
        in_specs=[yspec] * 5 + [pl.BlockSpec(w_bf16.shape, lambda i: (0, 0)),
                                pl.BlockSpec((tm, d), lambda i: (i, 0))],
        out_specs=pl.BlockSpec((tm, d), lambda i: (i, 0)),
        out_shape=jax.ShapeDtypeStruct((m, d), F32),
        compiler_params=_cparams(("parallel",)),
        name="outproj",
    )(*ys, w_bf16, x2)


def _sgu_kernel(u_ref, v_ref, z_ref, g_ref, w_ref, bias_ref, o_ref):
    t = u_ref.shape[0]
    u = jax.nn.gelu(u_ref[...])
    vn = _head_rms(jax.nn.gelu(v_ref[...]), g_ref[...])
    gate = jax.nn.silu(z_ref[...])
    tril = _tril(CHUNK)
    lo = _lo_mask((CHUNK, LANES))
    ws = [jnp.where(tril, w_ref[g], 0.0).astype(BF16) for g in range(GROUP_HEADS)]
    bias = bias_ref[...]
    for c in range(t // CHUNK):
        sl = slice(c * CHUNK, (c + 1) * CHUNK)
        tiles = []
        for p in range(2):
            vt = vn[sl, p * LANES:(p + 1) * LANES]
            tiles.append(_mm(ws[2 * p], jnp.where(lo, vt, 0.0)) + _mm(ws[2 * p + 1], jnp.where(lo, 0.0, vt)))
        mixed = jnp.concatenate(tiles, axis=1) + bias
        o_ref[sl, :] = (u[sl] * mixed * gate[sl]).astype(o_ref.dtype)


def _sgu(proj, g, w, bias_tile, t=512):
    m = proj.shape[0]
    col = lambda cb: pl.BlockSpec((t, GROUP_W), lambda i, cb=cb: (i, cb))
    full = lambda a: pl.BlockSpec(a.shape, lambda i, n=a.ndim: (0,) * n)
    return pl.pallas_call(
        _sgu_kernel,
        grid=(m // t,),
        in_specs=[col(CB["a_u"]), col(CB["a_v"]), col(CB["a_z"]), full(g), full(w), full(bias_tile)],
        out_specs=pl.BlockSpec((t, GROUP_W), lambda i: (i, 0)),
        out_shape=jax.ShapeDtypeStruct((m, GROUP_W), BF16),
        compiler_params=_cparams(("parallel",)),
        name="sgu",
    )(proj, proj, proj, g, w, bias_tile)


def _memkv_kernel(mem_ref, g_ref, w_ref, kg_ref, k_ref, v_ref):
    x = mem_ref[...]
    ms = jnp.mean(x * x, axis=-1, keepdims=True)
    h = (x * lax.rsqrt(ms + EPS) * g_ref[...]).astype(BF16)
    kv = jnp.dot(h, w_ref[...], preferred_element_type=F32)
    k_ref[...] = _head_rms(kv[:, :GROUP_W], kg_ref[...]).astype(BF16)
    v_ref[...] = kv[:, GROUP_W:].astype(BF16)


def _memkv(mem2, g, w_bf16, kg, n_mem):
    m, d = mem2.shape
    full = lambda a: pl.BlockSpec(a.shape, lambda i, n=a.ndim: (0,) * n)
    out = pl.BlockSpec((n_mem, GROUP_W), lambda i: (i, 0))
    return pl.pallas_call(
        _memkv_kernel,
        grid=(m // n_mem,),
        in_specs=[pl.BlockSpec((n_mem, d), lambda i: (i, 0)), full(g), full(w_bf16), full(kg)],
        out_specs=[out, out],
        out_shape=[jax.ShapeDtypeStruct((m, GROUP_W), BF16)] * 2,
        compiler_params=_cparams(("parallel",)),
        name="memkv",
    )(mem2, g, w_bf16, kg)


def _memattn_kernel(q_ref, z_ref, k_ref, v_ref, qg_ref, o_ref):
    q = (_head_rms(q_ref[...], qg_ref[...]) * SCALE).astype(BF16)
    k = k_ref[...]
    v = v_ref[...]
    t = q.shape[0]
    lo_q = _lo_mask((t, LANES))
    lo_v = _lo_mask((v.shape[0], LANES))
    zero = jnp.zeros((), BF16)
    outs = []
    for p in range(2):
        sl = slice(p * LANES, (p + 1) * LANES)
        qp, kp, vp = q[:, sl], k[:, sl], v[:, sl]
        acc = None
        for hh in range(2):
            mq = lo_q if hh == 0 else ~lo_q
            mv = lo_v if hh == 0 else ~lo_v
            s = _mm(jnp.where(mq, qp, zero), kp, NT)
            e = jnp.exp(s - jnp.max(s, axis=-1, keepdims=True))
            pr = e / jnp.sum(e, axis=-1, keepdims=True)
            pv = _mm(pr, jnp.where(mv, vp, zero))
            acc = pv if acc is None else acc + pv
        outs.append(acc)
    y = jnp.concatenate(outs, axis=1)
    o_ref[...] = (y * jax.nn.silu(z_ref[...])).astype(o_ref.dtype)


def _memattn(proj, kn, vb, qg, bsz, n_mem, t=512):
    m = proj.shape[0]
    nt = m // bsz // t
    col = lambda cb: pl.BlockSpec((t, GROUP_W), lambda b, i, cb=cb: (b * nt + i, cb))
    kv = pl.BlockSpec((n_mem, GROUP_W), lambda b, i: (b, 0))
    return pl.pallas_call(
        _memattn_kernel,
        grid=(bsz, nt),
        in_specs=[col(CB["m_q"]), col(CB["m_z"]), kv, kv, pl.BlockSpec(qg.shape, lambda b, i: (0, 0))],
        out_specs=pl.BlockSpec((t, GROUP_W), lambda b, i: (b * nt + i, 0)),
        out_shape=jax.ShapeDtypeStruct((m, GROUP_W), BF16),
        compiler_params=_cparams(("parallel", "parallel")),
        name="memattn",
    )(proj, proj, kn, vb, qg)


LOG2E = 1.4426950408889634
F_TERMS = 3


def _bf16_terms(x, n):
    terms, r = [], x
    for _ in range(n):
        p = r.astype(BF16).astype(F32)
        terms.append(p)
        r = r - p
    return terms


def _foxprep_kernel(q_ref, k_ref, v_ref, sm_ref, qg_ref, kg_ref, fb_ref,
                    kaug_ref, qaug_t_ref, v_t_ref, carry_ref):
    @pl.when(pl.program_id(1) == 0)
    def _():
        carry_ref[...] = jnp.zeros_like(carry_ref)

    t = q_ref.shape[0]
    qn = _head_rms(q_ref[...], qg_ref[...]) * (SCALE * LOG2E)
    kn = _head_rms(k_ref[...], kg_ref[...])
    lf = jax.nn.log_sigmoid(sm_ref[...] + fb_ref[...])
    cs = _mm_x(_tril(t).astype(BF16), lf, pb=3) + carry_ref[0:1, :]
    carry_ref[...] = jnp.broadcast_to(cs[t - 1:t, :], carry_ref.shape)
    f2 = cs * LOG2E

    lane = _iota((t, LANES), 1)
    q_tiles = []
    for h in range(GROUP_HEADS):
        sl = slice((h // 2) * LANES, (h // 2 + 1) * LANES)
        q_t, k_t = qn[:, sl], kn[:, sl]
        if h % 2 == 1:
            q_t = pltpu.roll(q_t, HEAD_DIM, axis=1)
            k_t = pltpu.roll(k_t, HEAD_DIM, axis=1)
        terms = _bf16_terms(f2[:, SM_BF + h:SM_BF + h + 1], F_TERMS)
        qa = jnp.where(lane < HEAD_DIM + 2 * F_TERMS, 1.0, 0.0)
        ka = qa
        for n, term in enumerate(terms):
            qa = jnp.where(lane == HEAD_DIM + n, term, qa)
            ka = jnp.where(lane == HEAD_DIM + F_TERMS + n, -term, ka)
        qa = jnp.where(lane < HEAD_DIM, q_t, qa)
        ka = jnp.where(lane < HEAD_DIM, k_t, ka)
        kaug_ref[:, h * LANES:(h + 1) * LANES] = ka.astype(BF16)
        q_tiles.append(qa)
    qaug_t_ref[0] = jnp.concatenate(q_tiles, axis=1).T.astype(BF16)
    v_t_ref[0] = v_ref[...].T.astype(BF16)


def _foxprep(proj, qg, kg, fb_row, bsz, t=512):
    m = proj.shape[0]
    seq = m // bsz
    nt = seq // t
    col = lambda cb: pl.BlockSpec((t, GROUP_W), lambda b, i, cb=cb: (b * nt + i, cb))
    row = lambda a: pl.BlockSpec(a.shape, lambda b, i: (0, 0))
    aug_w = GROUP_HEADS * LANES
    return pl.pallas_call(
        _foxprep_kernel,
        grid=(bsz, nt),
        in_specs=[col(CB["b_q"]), col(CB["b_k"]), col(CB["b_v"]),
                  pl.BlockSpec((t, LANES), lambda b, i: (b * nt + i, SMALL_CB)),
                  row(qg), row(kg), row(fb_row)],
        out_specs=[pl.BlockSpec((t, aug_w), lambda b, i: (b * nt + i, 0)),
                   pl.BlockSpec((1, aug_w, t), lambda b, i: (b, 0, i)),
                   pl.BlockSpec((1, GROUP_W, t), lambda b, i: (b, 0, i))],
        out_shape=[jax.ShapeDtypeStruct((m, aug_w), BF16),
                   jax.ShapeDtypeStruct((bsz, aug_w, seq), BF16),
                   jax.ShapeDtypeStruct((bsz, GROUP_W, seq), BF16)],
        scratch_shapes=[pltpu.VMEM((8, LANES), F32)],
        compiler_params=_cparams(("parallel", "arbitrary")),
        name="foxprep",
    )(proj, proj, proj, proj, qg, kg, fb_row)


def _fox_kernel(q_ref, k_ref, v_ref, z_ref, o_ref, m_scr, l_scr, acc_scr, sa_scr, sb_scr, mxa_scr, mxb_scr,
                *, tq, tk):
    i = pl.program_id(2)
    m_scr[...] = jnp.full_like(m_scr, -jnp.inf)
    l_scr[...] = jnp.zeros_like(l_scr)
    acc_scr[...] = jnp.zeros_like(acc_scr)

    def scores(j, s_scr, mx_scr):
        k0 = pl.multiple_of(j * tk, tk)
        for hh in range(2):
            k_blk = k_ref[pl.ds(k0, tk), hh * LANES:(hh + 1) * LANES]
            s_t = jnp.dot(k_blk, q_ref[0, hh * LANES:(hh + 1) * LANES, :],
                          preferred_element_type=F32)
            s_scr[hh] = s_t
            mx_scr[hh] = jnp.broadcast_to(jnp.max(s_t, axis=0, keepdims=True), (8, tq))

    def softmax_pv(j, s_scr, mx_scr, masked=False):
        k0 = pl.multiple_of(j * tk, tk)
        for hh in range(2):
            s_t = s_scr[hh]
            mx = mx_scr[hh][0:1, :]
            if masked:
                visible = (k0 + _iota((tk, tq), 0)) <= (i * tq + _iota((tk, tq), 1))
                s_t = jnp.where(visible, s_t, -jnp.inf)
                mx = jnp.max(s_t, axis=0, keepdims=True)
            m_old = m_scr[hh][0:1, :]
            m_new = jnp.maximum(m_old, mx)
            alpha = jnp.exp2(m_old - m_new)
            p_t = jnp.exp2(s_t - m_new)
            l_new = alpha * l_scr[hh][0:1, :] + jnp.sum(p_t, axis=0, keepdims=True)
            m_scr[hh] = jnp.broadcast_to(m_new, (8, tq))
            l_scr[hh] = jnp.broadcast_to(l_new, (8, tq))
            v_blk = v_ref[0, hh * HEAD_DIM:(hh + 1) * HEAD_DIM, pl.ds(k0, tk)]
            acc_scr[hh] = acc_scr[hh] * alpha + jnp.dot(v_blk, p_t.astype(BF16), preferred_element_type=F32)

    assert tq == 2 * tk
    n_full = 2 * i
    scores(0, sa_scr, mxa_scr)

    def two_blocks(jj, carry):
        scores(2 * jj + 1, sb_scr, mxb_scr)
        softmax_pv(2 * jj, sa_scr, mxa_scr)
        scores(2 * jj + 2, sa_scr, mxa_scr)
        softmax_pv(2 * jj + 1, sb_scr, mxb_scr)
        return carry

    lax.fori_loop(0, i, two_blocks, 0)
    scores(n_full + 1, sb_scr, mxb_scr)
    softmax_pv(n_full, sa_scr, mxa_scr, masked=True)
    softmax_pv(n_full + 1, sb_scr, mxb_scr, masked=True)

    y_t = jnp.concatenate([acc_scr[hh] / l_scr[hh][0:1, :] for hh in range(2)], axis=0)
    o_ref[...] = (y_t.T * jax.nn.silu(z_ref[...])).astype(o_ref.dtype)


def _fox(proj, kaug, qaug_t, v_t, bsz, tq=512, tk=256):
    m = proj.shape[0]
    seq = m // bsz
    nq = seq // tq
    s_buf = pltpu.VMEM((2, tk, tq), F32)
    mx_buf = pltpu.VMEM((2, 8, tq), F32)
    return pl.pallas_call(
        functools.partial(_fox_kernel, tq=tq, tk=tk),
        grid=(bsz, 2, nq),
        in_specs=[pl.BlockSpec((1, 2 * LANES, tq), lambda b, p, i: (b, p, i)),
                  pl.BlockSpec((seq, 2 * LANES), lambda b, p, i: (b, p)),
                  pl.BlockSpec((1, LANES, seq), lambda b, p, i: (b, p, 0)),
                  pl.BlockSpec((tq, LANES), lambda b, p, i: (b * nq + i, 2 * CB["b_z"] + p))],
        out_specs=pl.BlockSpec((tq, LANES), lambda b, p, i: (b * nq + i, p)),
        out_shape=jax.ShapeDtypeStruct((m, GROUP_W), BF16),
        scratch_shapes=[pltpu.VMEM((2, 8, tq), F32), pltpu.VMEM((2, 8, tq), F32),
                        pltpu.VMEM((2, HEAD_DIM, tq), F32), s_buf, s_buf, mx_buf, mx_buf],
        compiler_params=_cparams(("parallel", "parallel", "arbitrary")),
        name="fox",
    )(qaug_t, kaug, v_t, proj)


def _mlstm_kernel(q_ref, k_ref, v_ref, og_ref, z_ref, sm_ref, cw_ref, cb_ref, ib_ref, fb_ref, g_ref,
                  out_ref, xbuf, c_scr, n_scr, m_scr):
    t = CHUNK

    @pl.when(pl.program_id(1) == 0)
    def _():
        xbuf[0:8, :] = jnp.zeros((8, 2 * GROUP_W), F32)
        c_scr[...] = jnp.zeros_like(c_scr)
        n_scr[...] = jnp.zeros_like(n_scr)
        m_scr[...] = jnp.zeros_like(m_scr)

    xbuf[8:8 + t, 0:GROUP_W] = q_ref[...]
    xbuf[8:8 + t, GROUP_W:] = k_ref[...]
    conv = cb_ref[...]
    for jj in range(CONV_K):
        conv = conv + cw_ref[jj:jj + 1, :] * xbuf[8 - (CONV_K - 1) + jj:8 - (CONV_K - 1) + jj + t, :]
    xbuf[0:8, :] = xbuf[t:t + 8, :]
    qk = jax.nn.silu(conv)
    q = qk[:, :GROUP_W]
    k = qk[:, GROUP_W:] * SCALE
    v = v_ref[...]

    sm = sm_ref[...]
    li = sm + ib_ref[...]
    lf = jax.nn.log_sigmoid(sm + fb_ref[...])
    bcum = _mm_x(_tril(t).astype(BF16), lf, pb=3)
    b_t = bcum.T
    i_t = li.T

    tril = _tril(t)
    lo = _lo_mask((t, LANES))
    lo_row = _lo_mask((1, LANES))
    blockdiag = (_iota((LANES, LANES), 0) < HEAD_DIM) == (_iota((LANES, LANES), 1) < HEAD_DIM)
    lane_row = _iota((1, LANES), 1)
    m_row = m_scr[0:1, :]
    m_row_new = m_row
    h_tiles = []
    for p in range(2):
        sl = slice(p * LANES, (p + 1) * LANES)
        qp, kp, vp = q[:, sl], k[:, sl], v[:, sl]
        c_p = c_scr[p]
        n_p = n_scr[p][0:1, :]
        q_c = _mm(qp, c_p)
        q_n = qp * n_p
        h_tile = None
        w_cols, cds = [], []
        for hh in range(2):
            h = 2 * p + hh
            mask = lo if hh == 0 else ~lo
            b_c = bcum[:, SM_CF + h:SM_CF + h + 1]
            b_r = b_t[SM_CF + h:SM_CF + h + 1, :]
            i_c = li[:, SM_CI + h:SM_CI + h + 1]
            i_r = i_t[SM_CI + h:SM_CI + h + 1, :]
            m_st = m_row[:, h:h + 1]
            log_d = jnp.where(tril, b_c - b_r + i_r, -jnp.inf)
            inter = b_c + m_st
            m_t = jnp.maximum(jnp.max(log_d, axis=-1, keepdims=True), inter)
            s = _mm(jnp.where(mask, qp, 0.0), kp, NT) * jnp.exp(log_d - m_t)
            w_inter = jnp.exp(inter - m_t)
            num = _mm(s, jnp.where(mask, vp, 0.0)) + w_inter * jnp.where(mask, q_c, 0.0)
            den = (jnp.sum(s, axis=-1, keepdims=True)
                   + w_inter * jnp.sum(jnp.where(mask, q_n, 0.0), axis=-1, keepdims=True))
            hv = num / jnp.maximum(jnp.abs(den), jnp.exp(-m_t))
            h_tile = hv if h_tile is None else h_tile + hv
            g = b_c[t - 1:t, :]
            m_new = jnp.maximum(g + m_st, jnp.max(g - b_r + i_r, axis=-1, keepdims=True))
            w_cols.append(jnp.exp(g - b_c + i_c - m_new))
            cds.append(jnp.exp(g + m_st - m_new))
            m_row_new = jnp.where(lane_row == h, m_new, m_row_new)
        h_tiles.append(h_tile)
        kw = kp * jnp.where(lo, w_cols[0], w_cols[1])
        cd_row = jnp.where(lo_row, cds[0], cds[1])
        c_scr[p] = cd_row * c_p + jnp.where(blockdiag, _mm(kw, vp, TN), 0.0)
        n_new = cd_row * n_p + jnp.sum(kw, axis=0, keepdims=True)
        n_scr[p] = jnp.broadcast_to(n_new, (8, LANES))
    m_scr[...] = jnp.broadcast_to(m_row_new, m_scr.shape)

    hcat = jax.nn.sigmoid(og_ref[...]) * jnp.concatenate(h_tiles, axis=1)
    y = _head_rms(hcat, g_ref[...])
    out_ref[...] = (y * jax.nn.silu(z_ref[...])).astype(out_ref.dtype)


def _mlstm(proj, cw, cb, ib_row, fb_row, g, bsz):
    m = proj.shape[0]
    nc = m // bsz // CHUNK
    col = lambda cb_: pl.BlockSpec((CHUNK, GROUP_W), lambda b, c, cb_=cb_: (b * nc + c, cb_))
    full = lambda a: pl.BlockSpec(a.shape, lambda b, c, n=a.ndim: (0,) * n)
    return pl.pallas_call(
        _mlstm_kernel,
        grid=(bsz, nc),
        in_specs=[col(CB["c_q"]), col(CB["c_k"]), col(CB["c_v"]), col(CB["c_o"]), col(CB["c_z"]),
                  pl.BlockSpec((CHUNK, LANES), lambda b, c: (b * nc + c, SMALL_CB)),
                  full(cw), full(cb), full(ib_row), full(fb_row), full(g)],
        out_specs=pl.BlockSpec((CHUNK, GROUP_W), lambda b, c: (b * nc + c, 0)),
        out_shape=jax.ShapeDtypeStruct((m, GROUP_W), BF16),
        scratch_shapes=[pltpu.VMEM((CHUNK + 8, 2 * GROUP_W), F32), pltpu.VMEM((2, LANES, LANES), F32),
                        pltpu.VMEM((2, 8, LANES), F32), pltpu.VMEM((8, LANES), F32)],
        compiler_params=_cparams(("parallel", "arbitrary")),
        name="mlstm",
    )(proj, proj, proj, proj, proj, proj, cw, cb, ib_row, fb_row, g)


PASSES_AA = 1
PASSES_INV = 1
PASSES_STATE = 1


def _mmp(a, b, dims=NN, passes=1):
    return _mm_x(a, b, dims, pa=passes, pb=passes)


def _tri_inverse(a):
    n = a.shape[0]
    r, c = _iota((n, n), 0), _iota((n, n), 1)
    same = lambda b: (r // b) == (c // b)
    a8 = jnp.where(same(8), a, 0.0)
    t = jnp.where(r == c, 1.0, 0.0) + a8
    p = _mmp(a8, a8, passes=PASSES_INV)
    t = t + _mmp(t, p, passes=PASSES_INV)
    p = _mmp(p, p, passes=PASSES_INV)
    t = t + _mmp(t, p, passes=PASSES_INV)
    b = 8
    while b < n:
        a_off = jnp.where(same(2 * b) & ~same(b), a, 0.0)
        t = t + _mmp(t, _mmp(a_off, t, passes=PASSES_INV), passes=PASSES_INV)
        b *= 2
    return t


def _rwkv_prep_kernel(r_ref, k_ref, v_ref, z_ref, sm_ref, pr_ref, pk_ref, pv_ref, pz_ref, psm_ref,
                      mur_ref, muk_ref, muv_ref, muz_ref, mus_ref,
                      w0_ref, w2_ref, a0_ref, a2_ref, kk_ref, ka_ref, rk_ref,
                      als_ref, rs_ref, bh_ref, tg_ref, uind_ref, yind_ref, bonus_ref, gate_ref,
                      sconst_ref, wc_ref, *, nc):
    t = CHUNK
    first = (pl.program_id(0) % nc) == 0
    row0 = _iota((t, 1), 0) == 0

    def shifted(x_ref, p_ref, mu_ref):
        x = x_ref[...]
        prev_row = jnp.where(first, 0.0, p_ref[7:8, :])
        prev = jnp.where(row0, prev_row, pltpu.roll(x, 1, axis=0))
        return x + mu_ref[...] * (prev - x)

    r = shifted(r_ref, pr_ref, mur_ref)
    k = shifted(k_ref, pk_ref, muk_ref)
    v = shifted(v_ref, pv_ref, muv_ref)
    z = shifted(z_ref, pz_ref, muz_ref)
    sm = shifted(sm_ref, psm_ref, mus_ref)

    w_log = -jax.nn.softplus(-(w0_ref[...] + _mm(jnp.tanh(sm), w2_ref[...]))) - 0.5
    ld = -jnp.exp(w_log)
    a = jax.nn.sigmoid(a0_ref[...] + _mm(sm, a2_ref[...]))
    kk = k * kk_ref[...]
    kk = kk / jnp.maximum(jnp.sqrt(_head_sum(kk * kk)), 1e-12)
    k2 = k * (1.0 + (a - 1.0) * ka_ref[...])
    kka = kk * a

    lw = _mm_x(_tril(t).astype(BF16), ld, pb=3)
    lw_ex = lw - ld
    lw_mid = lw[t // 2 - 1:t // 2, :]
    lw_end = lw[t - 1:t, :]
    e_in = jnp.exp(lw - lw_mid)
    e_out = jnp.exp(lw_mid - lw)
    al_m = -kk * jnp.exp(lw_ex - lw_mid)
    r_m = r * e_in
    be_m = kka * e_out
    k_m = k2 * e_out
    al_s = -kk * jnp.exp(lw_ex)
    r_s = r * jnp.exp(lw)
    e_end = jnp.exp(lw_end - lw)
    b_h = kka * e_end
    k_h = k2 * e_end

    als_ref[...] = al_s.astype(BF16)
    rs_ref[...] = r_s.astype(BF16)
    bh_ref[...] = b_h.astype(BF16)
    bonus_ref[...] = _head_sum(r * k2 * rk_ref[...]) * v
    gate_ref[...] = jax.nn.silu(z)
    wc_ref[0] = jnp.broadcast_to(jnp.exp(lw_end), (8, GROUP_W))

    strict = _tril(t, strict=True)
    incl = _tril(t)
    lo = _lo_mask((t, LANES))
    for p in range(2):
        sl = slice(p * LANES, (p + 1) * LANES)
        rhs = jnp.concatenate([be_m[:, sl], k_m[:, sl]], axis=0)
        u_pair, y_pair, s_pair = None, None, None
        for hh in range(2):
            h = 2 * p + hh
            mask = lo if hh == 0 else ~lo
            lhs = jnp.concatenate([jnp.where(mask, al_m[:, sl], 0.0), jnp.where(mask, r_m[:, sl], 0.0)], axis=0)
            aa = _mmp(lhs, rhs, NT, passes=PASSES_AA)
            a_ab = jnp.where(strict, aa[:t, :t], 0.0)
            a_ak = jnp.where(strict, aa[:t, t:], 0.0)
            a_rb = jnp.where(incl, aa[t:, :t], 0.0)
            a_rk = jnp.where(incl, aa[t:, t:], 0.0)
            tinv = _tri_inverse(a_ab)
            v_h = jnp.where(mask, v[:, sl], 0.0)
            u_ind = _mmp(tinv, _mmp(a_ak, v_h, passes=PASSES_INV), passes=PASSES_INV)
            g_mat = _mmp(a_rb, tinv, passes=PASSES_INV)
            y_ind = _mmp(a_rb, u_ind, passes=PASSES_INV) + _mmp(a_rk, v_h, passes=PASSES_INV)
            s_c = (_mmp(u_ind, jnp.where(mask, b_h[:, sl], 0.0), TN, passes=PASSES_INV)
                   + _mmp(v_h, jnp.where(mask, k_h[:, sl], 0.0), TN, passes=PASSES_INV))
            tg_ref[:, h * 2 * t:h * 2 * t + t] = tinv.astype(BF16)
            tg_ref[:, h * 2 * t + t:(h + 1) * 2 * t] = g_mat.astype(BF16)
            u_pair = u_ind if u_pair is None else u_pair + u_ind
            y_pair = y_ind if y_pair is None else y_pair + y_ind
            s_pair = s_c if s_pair is None else s_pair + s_c
        uind_ref[:, sl] = u_pair
        yind_ref[:, sl] = y_pair
        sconst_ref[:, sl] = s_pair


def _rwkv_prep(proj, prm, bsz):
    m = proj.shape[0]
    nc = m // bsz // CHUNK
    sub = CHUNK // 8
    col = lambda cb: pl.BlockSpec((CHUNK, GROUP_W), lambda i, cb=cb: (i, cb))
    prev = lambda cb: pl.BlockSpec((8, GROUP_W), lambda i, cb=cb: (jnp.maximum(i * sub - 1, 0), cb))
    full = lambda a: pl.BlockSpec(a.shape, lambda i, n=a.ndim: (0,) * n)
    wide = pl.BlockSpec((CHUNK, GROUP_W), lambda i: (i, 0))
    params = [prm["mu_r"], prm["mu_k"], prm["mu_v"], prm["mu_z"], prm["mu_s"], prm["w0"], prm["w2p"],
              prm["a0"], prm["a2p"], prm["k_k"], prm["k_a"], prm["r_k"]]
    wide_bf = jax.ShapeDtypeStruct((m, GROUP_W), BF16)
    wide_f = jax.ShapeDtypeStruct((m, GROUP_W), F32)
    return pl.pallas_call(
        functools.partial(_rwkv_prep_kernel, nc=nc),
        grid=(m // CHUNK,),
        in_specs=[col(CB["d_r"]), col(CB["d_k"]), col(CB["d_v"]), col(CB["d_z"]),
                  pl.BlockSpec((CHUNK, LANES), lambda i: (i, SMALL_CB)),
                  prev(CB["d_r"]), prev(CB["d_k"]), prev(CB["d_v"]), prev(CB["d_z"]),
                  pl.BlockSpec((8, LANES), lambda i: (jnp.maximum(i * sub - 1, 0), SMALL_CB))]
        + [full(a) for a in params],
        out_specs=[wide, wide, wide,
                   pl.BlockSpec((CHUNK, 4 * GROUP_W), lambda i: (i, 0)),
                   wide, wide, wide, wide, wide,
                   pl.BlockSpec((1, 8, GROUP_W), lambda i: (i, 0, 0))],
        out_shape=[wide_bf, wide_bf, wide_bf, jax.ShapeDtypeStruct((m, 4 * GROUP_W), BF16),
                   wide_f, wide_f, wide_f, wide_f, wide_f,
                   jax.ShapeDtypeStruct((m // CHUNK, 8, GROUP_W), F32)],
        compiler_params=_cparams(("parallel",)),
        name="rwkv_prep",
    )(proj, proj, proj, proj, proj, proj, proj, proj, proj, proj, *params)


def _rwkv_scan_kernel(als_ref, rs_ref, bh_ref, tg_ref, yind_ref, bonus_ref, gate_ref, sconst_ref, wc_ref,
                      g_ref, out_ref, s_scr):
    t = CHUNK

    @pl.when(pl.program_id(1) == 0)
    def _():
        s_scr[...] = jnp.zeros_like(s_scr)

    lo = _lo_mask((t, LANES))
    blockdiag = (_iota((LANES, LANES), 0) < HEAD_DIM) == (_iota((LANES, LANES), 1) < HEAD_DIM)
    zero = jnp.zeros((), F32)
    y_tiles = []
    for p in range(2):
        sl = slice(p * LANES, (p + 1) * LANES)
        s_p = s_scr[p]
        lhs = jnp.concatenate([als_ref[:, sl], rs_ref[:, sl]], axis=0)
        x = _mm_x(lhs, s_p, NT, pb=PASSES_STATE)
        x0, rs0 = x[:t], x[t:]
        tx, gx = None, None
        for hh in range(2):
            h = 2 * p + hh
            mask = lo if hh == 0 else ~lo
            tg = jnp.concatenate([tg_ref[:, h * 2 * t:h * 2 * t + t],
                                  tg_ref[:, h * 2 * t + t:(h + 1) * 2 * t]], axis=0)
            res = _mm_x(tg, jnp.where(mask, x0, zero), pb=PASSES_STATE)
            tx = res[:t] if tx is None else tx + res[:t]
            gx = res[t:] if gx is None else gx + res[t:]
        y_tiles.append(rs0 + gx + yind_ref[:, sl])
        upd = _mm_x(tx, bh_ref[:, sl], TN, pa=PASSES_STATE)
        s_scr[p] = s_p * wc_ref[0][0:1, sl] + sconst_ref[:, sl] + jnp.where(blockdiag, upd, zero)
    y = _head_rms(jnp.concatenate(y_tiles, axis=1), g_ref[...]) + bonus_ref[...]
    out_ref[...] = (y * gate_ref[...]).astype(out_ref.dtype)


def _rwkv_scan(prep, g, bsz):
    als, rs, bh, tg, _uind, yind, bonus, gate, sconst, wc = prep
    m = als.shape[0]
    nc = m // bsz // CHUNK
    wide = pl.BlockSpec((CHUNK, GROUP_W), lambda b, c: (b * nc + c, 0))
    return pl.pallas_call(
        _rwkv_scan_kernel,
        grid=(bsz, nc),
        in_specs=[wide, wide, wide,
                  pl.BlockSpec((CHUNK, 4 * GROUP_W), lambda b, c: (b * nc + c, 0)),
                  wide, wide, wide, wide,
                  pl.BlockSpec((1, 8, GROUP_W), lambda b, c: (b * nc + c, 0, 0)),
                  pl.BlockSpec(g.shape, lambda b, c: (0, 0))],
        out_specs=wide,
        out_shape=jax.ShapeDtypeStruct((m, GROUP_W), BF16),
        scratch_shapes=[pltpu.VMEM((2, LANES, LANES), F32)],
        compiler_params=_cparams(("parallel", "arbitrary")),
        name="rwkv_scan",
    )(als, rs, bh, tg, yind, bonus, gate, sconst, wc, g)


def _row(v, width=None, offset=0):
    v = v.astype(F32).reshape(-1)
    width = v.shape[0] if width is None else width
    return jnp.zeros((1, width), F32).at[0, offset:offset + v.shape[0]].set(v)


def _layout_w_in(w_in):
    gw = GROUP_W
    a0 = 0
    b0 = a0 + 3 * gw
    c0 = b0 + 4 * gw + GROUP_HEADS
    d0 = c0 + 5 * gw + 2 * GROUP_HEADS
    m0 = d0 + 4 * gw + DECAY_RANK + A_RANK
    c = lambda s, w: w_in[:, s:s + w]
    wide = [c(a0, gw), c(a0 + gw, gw), c(a0 + 2 * gw, gw),
            c(b0, gw), c(b0 + gw, gw), c(b0 + 2 * gw, gw), c(b0 + 3 * gw + GROUP_HEADS, gw),
            c(c0, gw), c(c0 + gw, gw), c(c0 + 2 * gw, gw), c(c0 + 3 * gw + 2 * GROUP_HEADS, gw),
            c(c0 + 4 * gw + 2 * GROUP_HEADS, gw),
            c(d0, gw), c(d0 + gw + DECAY_RANK, gw), c(d0 + 2 * gw + DECAY_RANK, gw),
            c(d0 + 3 * gw + DECAY_RANK + A_RANK, gw),
            c(m0, gw), c(m0 + gw, gw)]
    small = jnp.zeros((w_in.shape[0], LANES), w_in.dtype)
    small = small.at[:, SM_BF:SM_BF + GROUP_HEADS].set(c(b0 + 3 * gw, GROUP_HEADS))
    small = small.at[:, SM_CI:SM_CI + GROUP_HEADS].set(c(c0 + 3 * gw, GROUP_HEADS))
    small = small.at[:, SM_CF:SM_CF + GROUP_HEADS].set(c(c0 + 3 * gw + GROUP_HEADS, GROUP_HEADS))
    small = small.at[:, SM_WLO:SM_WLO + DECAY_RANK].set(c(d0 + gw, DECAY_RANK))
    small = small.at[:, SM_ALO:SM_ALO + A_RANK].set(c(d0 + 3 * gw + DECAY_RANK, A_RANK))
    return jnp.concatenate(wide + [small], axis=1).astype(BF16)


def _layout_mu(mu):
    gw = GROUP_W
    o_w, o_k, o_v, o_a, o_z = gw, gw + DECAY_RANK, 2 * gw + DECAY_RANK, 3 * gw + DECAY_RANK, 3 * gw + DECAY_RANK + A_RANK
    mu_s = jnp.zeros((1, LANES), F32)
    mu_s = mu_s.at[0, SM_WLO:SM_WLO + DECAY_RANK].set(mu[o_w:o_w + DECAY_RANK])
    mu_s = mu_s.at[0, SM_ALO:SM_ALO + A_RANK].set(mu[o_a:o_a + A_RANK])
    return dict(mu_r=_row(mu[0:gw]), mu_k=_row(mu[o_k:o_k + gw]), mu_v=_row(mu[o_v:o_v + gw]),
                mu_z=_row(mu[o_z:o_z + gw]), mu_s=mu_s)


def _pad_rows(w, offset):
    return jnp.zeros((LANES, w.shape[1]), F32).at[offset:offset + w.shape[0]].set(w).astype(BF16)


def _layer(x2, mem2, bsz, n_mem, norm_g, w_in, w_out, sgu_norm_g, sgu_w, sgu_b, fox_q_g, fox_k_g, fox_f_b,
           mlstm_conv_w, mlstm_conv_b, mlstm_i_b, mlstm_f_b, mlstm_out_g,
           rwkv_mu, rwkv_w0, rwkv_w2, rwkv_a0, rwkv_a2, rwkv_k_k, rwkv_k_a, rwkv_r_k, rwkv_ln_g,
           mem_norm_g, mem_w_kv, mem_q_g, mem_k_g):
    tile_h = lambda g: _row(jnp.tile(g, GROUP_HEADS))
    proj = _inproj(x2, _row(norm_g), _layout_w_in(w_in))

    ya = _sgu(proj, _row(sgu_norm_g), sgu_w, jnp.repeat(sgu_b.T, HEAD_DIM, axis=1))

    kaug, qaug_t, v_t = _foxprep(proj, tile_h(fox_q_g), tile_h(fox_k_g), _row(fox_f_b, LANES, SM_BF), bsz)
    yb = _fox(proj, kaug, qaug_t, v_t, bsz)

    cw = jnp.zeros((8, 2 * GROUP_W), F32).at[:CONV_K].set(mlstm_conv_w)
    yc = _mlstm(proj, cw, _row(mlstm_conv_b), _row(mlstm_i_b, LANES, SM_CI), _row(mlstm_f_b, LANES, SM_CF),
                _row(mlstm_out_g), bsz)

    prm = _layout_mu(rwkv_mu)
    prm.update(w0=_row(rwkv_w0), w2p=_pad_rows(rwkv_w2, SM_WLO), a0=_row(rwkv_a0), a2p=_pad_rows(rwkv_a2, SM_ALO),
               k_k=_row(rwkv_k_k), k_a=_row(rwkv_k_a), r_k=_row(rwkv_r_k))
    yd = _rwkv_scan(_rwkv_prep(proj, prm, bsz), _row(rwkv_ln_g), bsz)

    mk, mv = _memkv(mem2, _row(mem_norm_g), mem_w_kv.astype(BF16), tile_h(mem_k_g), n_mem)
    ym = _memattn(proj, mk, mv, tile_h(mem_q_g), bsz, n_mem)

    return _outproj([ya, yb, yc, yd, ym], w_out.astype(BF16), x2)


def kernel(x, mem, norm_g, w_in, w_out, sgu_norm_g, sgu_w, sgu_b, fox_q_g, fox_k_g, fox_f_b, mlstm_conv_w, mlstm_conv_b, mlstm_i_b, mlstm_f_b, mlstm_out_g, rwkv_mu, rwkv_w0, rwkv_w2, rwkv_a0, rwkv_a2, rwkv_k_k, rwkv_k_a, rwkv_r_k, rwkv_ln_g, mem_norm_g, mem_w_kv, mem_q_g, mem_k_g):
    bsz, seq, d = x.shape
    n_mem = mem.shape[1]
    stacked = (norm_g, w_in, w_out, sgu_norm_g, sgu_w, sgu_b, fox_q_g, fox_k_g, fox_f_b,
               mlstm_conv_w, mlstm_conv_b, mlstm_i_b, mlstm_f_b, mlstm_out_g,
               rwkv_mu, rwkv_w0, rwkv_w2, rwkv_a0, rwkv_a2, rwkv_k_k, rwkv_k_a, rwkv_r_k, rwkv_ln_g,
               mem_norm_g, mem_w_kv, mem_q_g, mem_k_g)
    x2 = x.reshape(bsz * seq, d)
    mem2 = mem.reshape(bsz * n_mem, d)
    for layer in range(norm_g.shape[0]):
        x2 = _layer(x2, mem2, bsz, n_mem, *[p[layer] for p in stacked])
    return x2.reshape(bsz, seq, d)
```

```python
import functools

import jax
import jax.numpy as jnp
from jax import lax
from jax.experimental import pallas as pl
from jax.experimental.pallas import tpu as pltpu

F32 = jnp.float32
BF16 = jnp.bfloat16

HEAD_DIM = 64
GROUP_HEADS = 4
GROUP_W = GROUP_HEADS * HEAD_DIM
LANES = 128
CHUNK = 128
CONV_K = 4
EPS = 1e-6
SCALE = HEAD_DIM ** -0.5

CB = dict(a_u=0, a_v=1, a_z=2, b_q=3, b_k=4, b_v=5, b_z=6, c_q=7, c_k=8, c_v=9, c_o=10, c_z=11,
          d_r=12, d_k=13, d_v=14, d_z=15, m_q=16, m_z=17)
N_WIDE = 18 * GROUP_W
SMALL_CB = N_WIDE // LANES
NP = N_WIDE + LANES
SM_BF, SM_CI, SM_CF, SM_WLO, SM_ALO = 0, 4, 8, 16, 32
DECAY_RANK = 16
A_RANK = 16

VMEM_LIMIT = 48 * 1024 * 1024

NN = (((1,), (0,)), ((), ()))
NT = (((1,), (1,)), ((), ()))
TN = (((0,), (0,)), ((), ()))


def _mm(a, b, dims=NN):
    return lax.dot_general(a.astype(BF16), b.astype(BF16), dims, preferred_element_type=F32)


def _split(x, n):
    if x.dtype == BF16 or n == 1:
        return [x.astype(BF16)]
    parts, r = [], x
    for _ in range(n):
        p = r.astype(BF16)
        parts.append(p)
        r = r - p.astype(F32)
    return parts


def _mm_x(a, b, dims=NN, pa=1, pb=1):
    pa_list, pb_list = _split(a, pa), _split(b, pb)
    order = max(len(pa_list), len(pb_list))
    acc = None
    for i, ai in enumerate(pa_list):
        for j, bj in enumerate(pb_list):
            if i + j < order:
                t = lax.dot_general(ai, bj, dims, preferred_element_type=F32)
                acc = t if acc is None else acc + t
    return acc


def _iota(shape, axis):
    return lax.broadcasted_iota(jnp.int32, shape, axis)


def _tril(n, strict=False):
    r, c = _iota((n, n), 0), _iota((n, n), 1)
    return (c < r) if strict else (c <= r)


def _head_sum(x):
    w = x.shape[-1]
    bd = (_iota((w, w), 0) // HEAD_DIM == _iota((w, w), 1) // HEAD_DIM).astype(BF16)
    return _mm_x(x, bd, pa=3)


def _head_rms(x, g):
    ms = _head_sum(x * x) * (1.0 / HEAD_DIM)
    return x * lax.rsqrt(ms + EPS) * g


def _lo_mask(shape):
    return _iota(shape, len(shape) - 1) % LANES < HEAD_DIM


def _cparams(sem):
    return pltpu.CompilerParams(dimension_semantics=sem, vmem_limit_bytes=VMEM_LIMIT)


def _inproj_kernel(x_ref, g_ref, w_ref, o_ref, *, col_chunk):
    x = x_ref[...]
    ms = jnp.mean(x * x, axis=-1, keepdims=True)
    h = (x * lax.rsqrt(ms + EPS) * g_ref[...]).astype(BF16)
    n = o_ref.shape[1]
    for c0 in range(0, n, col_chunk):
        c1 = min(c0 + col_chunk, n)
        o_ref[:, c0:c1] = jnp.dot(h, w_ref[:, c0:c1], preferred_element_type=F32)


def _inproj(x2, g, w_bf16, tm=256):
    m, d = x2.shape
    n = w_bf16.shape[1]
    return pl.pallas_call(
        functools.partial(_inproj_kernel, col_chunk=512),
        grid=(m // tm,),
        in_specs=[pl.BlockSpec((tm, d), lambda i: (i, 0)),
                  pl.BlockSpec((1, d), lambda i: (0, 0)),
                  pl.BlockSpec((d, n), lambda i: (0, 0))],
        out_specs=pl.BlockSpec((tm, n), lambda i: (i, 0)),
        out_shape=jax.ShapeDtypeStruct((m, n), F32),
        compiler_params=_cparams(("parallel",)),
        name="inproj",
    )(x2, g, w_bf16)


def _outproj_kernel(ya, yb, yc, yd, ym, w_ref, x_ref, o_ref):
    mixed = jnp.concatenate([ya[...], yb[...], yc[...], yd[...], ym[...]], axis=1)
    o_ref[...] = x_ref[...] + jnp.dot(mixed, w_ref[...], preferred_element_type=F32)


def _outproj(ys, w_bf16, x2, tm=512):
    m, d = x2.shape
    yspec = pl.BlockSpec((tm, GROUP_W), lambda i: (i, 0))
    return pl.pallas_call(
        _outproj_kernel,
        grid=(m // tm,),
        in_specs=[yspec] * 5 + [pl.BlockSpec(w_bf16.shape, lambda i: (0, 0)),
                                pl.BlockSpec((tm, d), lambda i: (i, 0))],
        out_specs=pl.BlockSpec((tm, d), lambda i: (i, 0)),
        out_shape=jax.ShapeDtypeStruct((m, d), F32),
        compiler_params=_cparams(("parallel",)),
        name="outproj",
    )(*ys, w_bf16, x2)


def _sgu_kernel(u_ref, v_ref, z_ref, g_ref, w_ref, bias_ref, o_ref):
    t = u_ref.shape[0]
    u = jax.nn.gelu(u_ref[...])
    vn = _head_rms(jax.nn.gelu(v_ref[...]), g_ref[...])
    gate = jax.nn.silu(z_ref[...])
    tril = _tril(CHUNK)
    lo = _lo_mask((CHUNK, LANES))
    ws = [jnp.where(tril, w_ref[g], 0.0).astype(BF16) for g in range(GROUP_HEADS)]
    bias = bias_ref[...]
    for c in range(t // CHUNK):
        sl = slice(c * CHUNK, (c + 1) * CHUNK)
        tiles = []
        for p in range(2):
            vt = vn[sl, p * LANES:(p + 1) * LANES]
            tiles.append(_mm(ws[2 * p], jnp.where(lo, vt, 0.0)) + _mm(ws[2 * p + 1], jnp.where(lo, 0.0, vt)))
        mixed = jnp.concatenate(tiles, axis=1) + bias
        o_ref[sl, :] = (u[sl] * mixed * gate[sl]).astype(o_ref.dtype)


def _sgu(proj, g, w, bias_tile, t=512):
    m = proj.shape[0]
    col = lambda cb: pl.BlockSpec((t, GROUP_W), lambda i, cb=cb: (i, cb))
    full = lambda a: pl.BlockSpec(a.shape, lambda i, n=a.ndim: (0,) * n)
    return pl.pallas_call(
        _sgu_kernel,
        grid=(m // t,),
        in_specs=[col(CB["a_u"]), col(CB["a_v"]), col(CB["a_z"]), full(g), full(w), full(bias_tile)],
        out_specs=pl.BlockSpec((t, GROUP_W), lambda i: (i, 0)),
        out_shape=jax.ShapeDtypeStruct((m, GROUP_W), BF16),
        compiler_params=_cparams(("parallel",)),
        name="sgu",
    )(proj, proj, proj, g, w, bias_tile)


def _memkv_kernel(mem_ref, g_ref, w_ref, kg_ref, k_ref, v_ref):
    x = mem_ref[...]
    ms = jnp.mean(x * x, axis=-1, keepdims=True)
    h = (x * lax.rsqrt(ms + EPS) * g_ref[...]).astype(BF16)
    kv = jnp.dot(h, w_ref[...], preferred_element_type=F32)
    k_ref[...] = _head_rms(kv[:, :GROUP_W], kg_ref[...]).astype(BF16)
    v_ref[...] = kv[:, GROUP_W:].astype(BF16)


def _memkv(mem2, g, w_bf16, kg, n_mem):
    m, d = mem2.shape
    full = lambda a: pl.BlockSpec(a.shape, lambda i, n=a.ndim: (0,) * n)
    out = pl.BlockSpec((n_mem, GROUP_W), lambda i: (i, 0))
    return pl.pallas_call(
        _memkv_kernel,
        grid=(m // n_mem,),
        in_specs=[pl.BlockSpec((n_mem, d), lambda i: (i, 0)), full(g), full(w_bf16), full(kg)],
        out_specs=[out, out],
        out_shape=[jax.ShapeDtypeStruct((m, GROUP_W), BF16)] * 2,
        compiler_params=_cparams(("parallel",)),
        name="memkv",
    )(mem2, g, w_bf16, kg)


def _memattn_kernel(q_ref, z_ref, k_ref, v_ref, qg_ref, o_ref):
    q = (_head_rms(q_ref[...], qg_ref[...]) * SCALE).astype(BF16)
    k = k_ref[...]
    v = v_ref[...]
    t = q.shape[0]
    lo_q = _lo_mask((t, LANES))
    lo_v = _lo_mask((v.shape[0], LANES))
    zero = jnp.zeros((), BF16)
    outs = []
    for p in range(2):
        sl = slice(p * LANES, (p + 1) * LANES)
        qp, kp, vp = q[:, sl], k[:, sl], v[:, sl]
        acc = None
        for hh in range(2):
            mq = lo_q if hh == 0 else ~lo_q
            mv = lo_v if hh == 0 else ~lo_v
            s = _mm(jnp.where(mq, qp, zero), kp, NT)
            e = jnp.exp(s - jnp.max(s, axis=-1, keepdims=True))
            pr = e / jnp.sum(e, axis=-1, keepdims=True)
            pv = _mm(pr, jnp.where(mv, vp, zero))
            acc = pv if acc is None else acc + pv
        outs.append(acc)
    y = jnp.concatenate(outs, axis=1)
    o_ref[...] = (y * jax.nn.silu(z_ref[...])).astype(o_ref.dtype)


def _memattn(proj, kn, vb, qg, bsz, n_mem, t=512):
    m = proj.shape[0]
    nt = m // bsz // t
    col = lambda cb: pl.BlockSpec((t, GROUP_W), lambda b, i, cb=cb: (b * nt + i, cb))
    kv = pl.BlockSpec((n_mem, GROUP_W), lambda b, i: (b, 0))
    return pl.pallas_call(
        _memattn_kernel,
        grid=(bsz, nt),
        in_specs=[col(CB["m_q"]), col(CB["m_z"]), kv, kv, pl.BlockSpec(qg.shape, lambda b, i: (0, 0))],
        out_specs=pl.BlockSpec((t, GROUP_W), lambda b, i: (b * nt + i, 0)),
        out_shape=jax.ShapeDtypeStruct((m, GROUP_W), BF16),
        compiler_params=_cparams(("parallel", "parallel")),
        name="memattn",
    )(proj, proj, kn, vb, qg)


LOG2E = 1.4426950408889634
F_TERMS = 3


def _bf16_terms(x, n):
    terms, r = [], x
    for _ in range(n):
        p = r.astype(BF16).astype(F32)
        terms.append(p)
        r = r - p
    return terms


def _foxprep_kernel(q_ref, k_ref, v_ref, sm_ref, qg_ref, kg_ref, fb_ref,
                    kaug_ref, qaug_t_ref, v_t_ref, carry_ref):
    @pl.when(pl.program_id(1) == 0)
    def _():
        carry_ref[...] = jnp.zeros_like(carry_ref)

    t = q_ref.shape[0]
    qn = _head_rms(q_ref[...], qg_ref[...]) * (SCALE * LOG2E)
    kn = _head_rms(k_ref[...], kg_ref[...])
    lf = jax.nn.log_sigmoid(sm_ref[...] + fb_ref[...])
    cs = _mm_x(_tril(t).astype(BF16), lf, pb=3) + carry_ref[0:1, :]
    carry_ref[...] = jnp.broadcast_to(cs[t - 1:t, :], carry_ref.shape)
    f2 = cs * LOG2E

    lane = _iota((t, LANES), 1)
    q_tiles = []
    for h in range(GROUP_HEADS):
        sl = slice((h // 2) * LANES, (h // 2 + 1) * LANES)
        q_t, k_t = qn[:, sl], kn[:, sl]
        if h % 2 == 1:
            q_t = pltpu.roll(q_t, HEAD_DIM, axis=1)
            k_t = pltpu.roll(k_t, HEAD_DIM, axis=1)
        terms = _bf16_terms(f2[:, SM_BF + h:SM_BF + h + 1], F_TERMS)
        qa = jnp.where(lane < HEAD_DIM + 2 * F_TERMS, 1.0, 0.0)
        ka = qa
        for n, term in enumerate(terms):
            qa = jnp.where(lane == HEAD_DIM + n, term, qa)
            ka = jnp.where(lane == HEAD_DIM + F_TERMS + n, -term, ka)
        qa = jnp.where(lane < HEAD_DIM, q_t, qa)
        ka = jnp.where(lane < HEAD_DIM, k_t, ka)
        kaug_ref[:, h * LANES:(h + 1) * LANES] = ka.astype(BF16)
        q_tiles.append(qa)
    qaug_t_ref[0] = jnp.concatenate(q_tiles, axis=1).T.astype(BF16)
    v_t_ref[0] = v_ref[...].T.astype(BF16)


def _foxprep(proj, qg, kg, fb_row, bsz, t=512):
    m = proj.shape[0]
    seq = m // bsz
    nt = seq // t
    col = lambda cb: pl.BlockSpec((t, GROUP_W), lambda b, i, cb=cb: (b * nt + i, cb))
    row = lambda a: pl.BlockSpec(a.shape, lambda b, i: (0, 0))
    aug_w = GROUP_HEADS * LANES
    return pl.pallas_call(
        _foxprep_kernel,
        grid=(bsz, nt),
        in_specs=[col(CB["b_q"]), col(CB["b_k"]), col(CB["b_v"]),
                  pl.BlockSpec((t, LANES), lambda b, i: (b * nt + i, SMALL_CB)),
                  row(qg), row(kg), row(fb_row)],
        out_specs=[pl.BlockSpec((t, aug_w), lambda b, i: (b * nt + i, 0)),
                   pl.BlockSpec((1, aug_w, t), lambda b, i: (b, 0, i)),
                   pl.BlockSpec((1, GROUP_W, t), lambda b, i: (b, 0, i))],
        out_shape=[jax.ShapeDtypeStruct((m, aug_w), BF16),
                   jax.ShapeDtypeStruct((bsz, aug_w, seq), BF16),
                   jax.ShapeDtypeStruct((bsz, GROUP_W, seq), BF16)],
        scratch_shapes=[pltpu.VMEM((8, LANES), F32)],
        compiler_params=_cparams(("parallel", "arbitrary")),
        name="foxprep",
    )(proj, proj, proj, proj, qg, kg, fb_row)


def _fox_kernel(q_ref, k_ref, v_ref, z_ref, o_ref, m_scr, l_scr, acc_scr, sa_scr, sb_scr, mxa_scr, mxb_scr,
                *, tq, tk):
    i = pl.program_id(2)
    m_scr[...] = jnp.full_like(m_scr, -jnp.inf)
    l_scr[...] = jnp.zeros_like(l_scr)
    acc_scr[...] = jnp.zeros_like(acc_scr)

    def scores(j, s_scr, mx_scr):
        k0 = pl.multiple_of(j * tk, tk)
        for hh in range(2):
            k_blk = k_ref[pl.ds(k0, tk), hh * LANES:(hh + 1) * LANES]
            s_t = jnp.dot(k_blk, q_ref[0, hh * LANES:(hh + 1) * LANES, :],
                          preferred_element_type=F32)
            s_scr[hh] = s_t
            mx_scr[hh] = jnp.broadcast_to(jnp.max(s_t, axis=0, keepdims=True), (8, tq))

    def softmax_pv(j, s_scr, mx_scr, masked=False):
        k0 = pl.multiple_of(j * tk, tk)
        for hh in range(2):
            s_t = s_scr[hh]
            mx = mx_scr[hh][0:1, :]
            if masked:
                visible = (k0 + _iota((tk, tq), 0)) <= (i * tq + _iota((tk, tq), 1))
                s_t = jnp.where(visible, s_t, -jnp.inf)
                mx = jnp.max(s_t, axis=0, keepdims=True)
            m_old = m_scr[hh][0:1, :]
            m_new = jnp.maximum(m_old, mx)
            alpha = jnp.exp2(m_old - m_new)
            p_t = jnp.exp2(s_t - m_new)
            l_new = alpha * l_scr[hh][0:1, :] + jnp.sum(p_t, axis=0, keepdims=True)
            m_scr[hh] = jnp.broadcast_to(m_new, (8, tq))
            l_scr[hh] = jnp.broadcast_to(l_new, (8, tq))
            v_blk = v_ref[0, hh * HEAD_DIM:(hh + 1) * HEAD_DIM, pl.ds(k0, tk)]
            acc_scr[hh] = acc_scr[hh] * alpha + jnp.dot(v_blk, p_t.astype(BF16), preferred_element_type=F32)

    assert tq == 2 * tk
    n_full = 2 * i
    scores(0, sa_scr, mxa_scr)

    def two_blocks(jj, carry):
        scores(2 * jj + 1, sb_scr, mxb_scr)
        softmax_pv(2 * jj, sa_scr, mxa_scr)
        scores(2 * jj + 2, sa_scr, mxa_scr)
        softmax_pv(2 * jj + 1, sb_scr, mxb_scr)
        return carry

    lax.fori_loop(0, i, two_blocks, 0)
    scores(n_full + 1, sb_scr, mxb_scr)
    softmax_pv(n_full, sa_scr, mxa_scr, masked=True)
    softmax_pv(n_full + 1, sb_scr, mxb_scr, masked=True)

    y_t = jnp.concatenate([acc_scr[hh] / l_scr[hh][0:1, :] for hh in range(2)], axis=0)
    o_ref[...] = (y_t.T * jax.nn.silu(z_ref[...])).astype(o_ref.dtype)


def _fox(proj, kaug, qaug_t, v_t, bsz, tq=512, tk=256):
    m = proj.shape[0]
    seq = m // bsz
    nq = seq // tq
    s_buf = pltpu.VMEM((2, tk, tq), F32)
    mx_buf = pltpu.VMEM((2, 8, tq), F32)
    return pl.pallas_call(
        functools.partial(_fox_kernel, tq=tq, tk=tk),
        grid=(bsz, 2, nq),
        in_specs=[pl.BlockSpec((1, 2 * LANES, tq), lambda b, p, i: (b, p, i)),
                  pl.BlockSpec((seq, 2 * LANES), lambda b, p, i: (b, p)),
                  pl.BlockSpec((1, LANES, seq), lambda b, p, i: (b, p, 0)),
                  pl.BlockSpec((tq, LANES), lambda b, p, i: (b * nq + i, 2 * CB["b_z"] + p))],
        out_specs=pl.BlockSpec((tq, LANES), lambda b, p, i: (b * nq + i, p)),
        out_shape=jax.ShapeDtypeStruct((m, GROUP_W), BF16),
        scratch_shapes=[pltpu.VMEM((2, 8, tq), F32), pltpu.VMEM((2, 8, tq), F32),
                        pltpu.VMEM((2, HEAD_DIM, tq), F32), s_buf, s_buf, mx_buf, mx_buf],
        compiler_params=_cparams(("parallel", "parallel", "arbitrary")),
        name="fox",
    )(qaug_t, kaug, v_t, proj)


def _mlstm_kernel(q_ref, k_ref, v_ref, og_ref, z_ref, sm_ref, cw_ref, cb_ref, ib_ref, fb_ref, g_ref,
                  out_ref, xbuf, c_scr, n_scr, m_scr):
    t = CHUNK

    @pl.when(pl.program_id(1) == 0)
    def _():
        xbuf[0:8, :] = jnp.zeros((8, 2 * GROUP_W), F32)
        c_scr[...] = jnp.zeros_like(c_scr)
        n_scr[...] = jnp.zeros_like(n_scr)
        m_scr[...] = jnp.zeros_like(m_scr)

    xbuf[8:8 + t, 0:GROUP_W] = q_ref[...]
    xbuf[8:8 + t, GROUP_W:] = k_ref[...]
    conv = cb_ref[...]
    for jj in range(CONV_K):
        conv = conv + cw_ref[jj:jj + 1, :] * xbuf[8 - (CONV_K - 1) + jj:8 - (CONV_K - 1) + jj + t, :]
    xbuf[0:8, :] = xbuf[t:t + 8, :]
    qk = jax.nn.silu(conv)
    q = qk[:, :GROUP_W]
    k = qk[:, GROUP_W:] * SCALE
    v = v_ref[...]

    sm = sm_ref[...]
    li = sm + ib_ref[...]
    lf = jax.nn.log_sigmoid(sm + fb_ref[...])
    bcum = _mm_x(_tril(t).astype(BF16), lf, pb=3)
    b_t = bcum.T
    i_t = li.T

    tril = _tril(t)
    lo = _lo_mask((t, LANES))
    lo_row = _lo_mask((1, LANES))
    blockdiag = (_iota((LANES, LANES), 0) < HEAD_DIM) == (_iota((LANES, LANES), 1) < HEAD_DIM)
    lane_row = _iota((1, LANES), 1)
    m_row = m_scr[0:1, :]
    heads = range(GROUP_HEADS)
    sls = [slice(p * LANES, (p + 1) * LANES) for p in range(2)]
    qs, ks, vs = [q[:, s_] for s_ in sls], [k[:, s_] for s_ in sls], [v[:, s_] for s_ in sls]
    c_ps = [c_scr[p] for p in range(2)]
    n_ps = [n_scr[p][0:1, :] for p in range(2)]
    masks = [lo if h % 2 == 0 else ~lo for h in heads]
    qk = [_mm(jnp.where(masks[h], qs[h // 2], 0.0), ks[h // 2], NT) for h in heads]
    q_c = [_mm(qs[p], c_ps[p]) for p in range(2)]
    q_n = [qs[p] * n_ps[p] for p in range(2)]
    b_c = [bcum[:, SM_CF + h:SM_CF + h + 1] for h in heads]
    b_r = [b_t[SM_CF + h:SM_CF + h + 1, :] for h in heads]
    i_c = [li[:, SM_CI + h:SM_CI + h + 1] for h in heads]
    i_r = [i_t[SM_CI + h:SM_CI + h + 1, :] for h in heads]
    m_st = [m_row[:, h:h + 1] for h in heads]
    log_d = [jnp.where(tril, b_c[h] - b_r[h] + i_r[h], -jnp.inf) for h in heads]
    inter = [b_c[h] + m_st[h] for h in heads]
    m_t = [jnp.maximum(jnp.max(log_d[h], axis=-1, keepdims=True), inter[h]) for h in heads]
    s = [qk[h] * jnp.exp(log_d[h] - m_t[h]) for h in heads]
    w_inter = [jnp.exp(inter[h] - m_t[h]) for h in heads]
    sv = [_mm(s[h], jnp.where(masks[h], vs[h // 2], 0.0)) for h in heads]
    num = [sv[h] + w_inter[h] * jnp.where(masks[h], q_c[h // 2], 0.0) for h in heads]
    den = [jnp.sum(s[h], axis=-1, keepdims=True)
           + w_inter[h] * jnp.sum(jnp.where(masks[h], q_n[h // 2], 0.0), axis=-1, keepdims=True) for h in heads]
    hv = [num[h] / jnp.maximum(jnp.abs(den[h]), jnp.exp(-m_t[h])) for h in heads]
    g = [b_c[h][t - 1:t, :] for h in heads]
    m_new = [jnp.maximum(g[h] + m_st[h], jnp.max(g[h] - b_r[h] + i_r[h], axis=-1, keepdims=True)) for h in heads]
    w_col = [jnp.exp(g[h] - b_c[h] + i_c[h] - m_new[h]) for h in heads]
    cd = [jnp.exp(g[h] + m_st[h] - m_new[h]) for h in heads]
    m_row_new = m_row
    for h in heads:
        m_row_new = jnp.where(lane_row == h, m_new[h], m_row_new)
    kw = [ks[p] * jnp.where(lo, w_col[2 * p], w_col[2 * p + 1]) for p in range(2)]
    cd_row = [jnp.where(lo_row, cd[2 * p], cd[2 * p + 1]) for p in range(2)]
    upd = [_mm(kw[p], vs[p], TN) for p in range(2)]
    for p in range(2):
        c_scr[p] = cd_row[p] * c_ps[p] + jnp.where(blockdiag, upd[p], 0.0)
        n_new = cd_row[p] * n_ps[p] + jnp.sum(kw[p], axis=0, keepdims=True)
        n_scr[p] = jnp.broadcast_to(n_new, (8, LANES))
    m_scr[...] = jnp.broadcast_to(m_row_new, m_scr.shape)
    h_tiles = [hv[2 * p] + hv[2 * p + 1] for p in range(2)]

    hcat = jax.nn.sigmoid(og_ref[...]) * jnp.concatenate(h_tiles, axis=1)
    y = _head_rms(hcat, g_ref[...])
    out_ref[...] = (y * jax.nn.silu(z_ref[...])).astype(out_ref.dtype)


def _mlstm(proj, cw, cb, ib_row, fb_row, g, bsz):
    m = proj.shape[0]
    nc = m // bsz // CHUNK
    col = lambda cb_: pl.BlockSpec((CHUNK, GROUP_W), lambda b, c, cb_=cb_: (b * nc + c, cb_))
    full = lambda a: pl.BlockSpec(a.shape, lambda b, c, n=a.ndim: (0,) * n)
    return pl.pallas_call(
        _mlstm_kernel,
        grid=(bsz, nc),
        in_specs=[col(CB["c_q"]), col(CB["c_k"]), col(CB["c_v"]), col(CB["c_o"]), col(CB["c_z"]),
                  pl.BlockSpec((CHUNK, LANES), lambda b, c: (b * nc + c, SMALL_CB)),
                  full(cw), full(cb), full(ib_row), full(fb_row), full(g)],
        out_specs=pl.BlockSpec((CHUNK, GROUP_W), lambda b, c: (b * nc + c, 0)),
        out_shape=jax.ShapeDtypeStruct((m, GROUP_W), BF16),
        scratch_shapes=[pltpu.VMEM((CHUNK + 8, 2 * GROUP_W), F32), pltpu.VMEM((2, LANES, LANES), F32),
                        pltpu.VMEM((2, 8, LANES), F32), pltpu.VMEM((8, LANES), F32)],
        compiler_params=_cparams(("parallel", "arbitrary")),
        name="mlstm",
    )(proj, proj, proj, proj, proj, proj, cw, cb, ib_row, fb_row, g)


PASSES_AA = 1
PASSES_INV = 1
PASSES_STATE = 1


def _mmp(a, b, dims=NN, passes=1):
    return _mm_x(a, b, dims, pa=passes, pb=passes)


def _tri_inverse(mats):
    n = mats[0].shape[0]
    r, c = _iota((n, n), 0), _iota((n, n), 1)
    same = lambda b: (r // b) == (c // b)
    mm = lambda x, y: _mmp(x, y, passes=PASSES_INV)
    eye = jnp.where(r == c, 1.0, 0.0)
    a8 = [jnp.where(same(8), a, 0.0) for a in mats]
    ts = [eye + x for x in a8]
    ps = [mm(x, x) for x in a8]
    ts = [t + mm(t, p) for t, p in zip(ts, ps)]
    ps = [mm(p, p) for p in ps]
    ts = [t + mm(t, p) for t, p in zip(ts, ps)]
    b = 8
    while b < n:
        off = same(2 * b) & ~same(b)
        inner = [mm(jnp.where(off, a, 0.0), t) for a, t in zip(mats, ts)]
        ts = [t + mm(t, x) for t, x in zip(ts, inner)]
        b *= 2
    return ts


def _rwkv_prep_kernel(r_ref, k_ref, v_ref, z_ref, sm_ref, pr_ref, pk_ref, pv_ref, pz_ref, psm_ref,
                      mur_ref, muk_ref, muv_ref, muz_ref, mus_ref,
                      w0_ref, w2_ref, a0_ref, a2_ref, kk_ref, ka_ref, rk_ref,
                      als_ref, rs_ref, bh_ref, tg_ref, uind_ref, yind_ref, bonus_ref, gate_ref,
                      sconst_ref, wc_ref, *, nc):
    t = CHUNK
    first = (pl.program_id(0) % nc) == 0
    row0 = _iota((t, 1), 0) == 0

    def shifted(x_ref, p_ref, mu_ref):
        x = x_ref[...]
        prev_row = jnp.where(first, 0.0, p_ref[7:8, :])
        prev = jnp.where(row0, prev_row, pltpu.roll(x, 1, axis=0))
        return x + mu_ref[...] * (prev - x)

    r = shifted(r_ref, pr_ref, mur_ref)
    k = shifted(k_ref, pk_ref, muk_ref)
    v = shifted(v_ref, pv_ref, muv_ref)
    z = shifted(z_ref, pz_ref, muz_ref)
    sm = shifted(sm_ref, psm_ref, mus_ref)

    w_log = -jax.nn.softplus(-(w0_ref[...] + _mm(jnp.tanh(sm), w2_ref[...]))) - 0.5
    ld = -jnp.exp(w_log)
    a = jax.nn.sigmoid(a0_ref[...] + _mm(sm, a2_ref[...]))
    kk = k * kk_ref[...]
    kk = kk / jnp.maximum(jnp.sqrt(_head_sum(kk * kk)), 1e-12)
    k2 = k * (1.0 + (a - 1.0) * ka_ref[...])
    kka = kk * a

    lw = _mm_x(_tril(t).astype(BF16), ld, pb=3)
    lw_ex = lw - ld
    lw_mid = lw[t // 2 - 1:t // 2, :]
    lw_end = lw[t - 1:t, :]
    e_in = jnp.exp(lw - lw_mid)
    e_out = jnp.exp(lw_mid - lw)
    al_m = -kk * jnp.exp(lw_ex - lw_mid)
    r_m = r * e_in
    be_m = kka * e_out
    k_m = k2 * e_out
    al_s = -kk * jnp.exp(lw_ex)
    r_s = r * jnp.exp(lw)
    e_end = jnp.exp(lw_end - lw)
    b_h = kka * e_end
    k_h = k2 * e_end

    als_ref[...] = al_s.astype(BF16)
    rs_ref[...] = r_s.astype(BF16)
    bh_ref[...] = b_h.astype(BF16)
    bonus_ref[...] = _head_sum(r * k2 * rk_ref[...]) * v
    gate_ref[...] = jax.nn.silu(z)
    wc_ref[0] = jnp.broadcast_to(jnp.exp(lw_end), (8, GROUP_W))

    strict = _tril(t, strict=True)
    incl = _tril(t)
    lo = _lo_mask((t, LANES))
    heads = range(GROUP_HEADS)
    sls = [slice((h // 2) * LANES, (h // 2 + 1) * LANES) for h in heads]
    masks = [lo if h % 2 == 0 else ~lo for h in heads]
    mm = lambda x, y, dims=NN: _mmp(x, y, dims, passes=PASSES_INV)
    sel = lambda h, x: jnp.where(masks[h], x[:, sls[h]], 0.0)
    aa = [_mmp(jnp.concatenate([sel(h, al_m), sel(h, r_m)], axis=0),
               jnp.concatenate([be_m[:, sls[h]], k_m[:, sls[h]]], axis=0), NT, passes=PASSES_AA) for h in heads]
    a_ab = [jnp.where(strict, x[:t, :t], 0.0) for x in aa]
    a_ak = [jnp.where(strict, x[:t, t:], 0.0) for x in aa]
    a_rb = [jnp.where(incl, x[t:, :t], 0.0) for x in aa]
    a_rk = [jnp.where(incl, x[t:, t:], 0.0) for x in aa]
    v_h = [sel(h, v) for h in heads]
    av = [mm(a_ak[h], v_h[h]) for h in heads]
    y_rk = [mm(a_rk[h], v_h[h]) for h in heads]
    s_vk = [mm(v_h[h], sel(h, k_h), TN) for h in heads]
    tinv = _tri_inverse(a_ab)
    u_ind = [mm(tinv[h], av[h]) for h in heads]
    g_mat = [mm(a_rb[h], tinv[h]) for h in heads]
    y_ind = [mm(a_rb[h], u_ind[h]) + y_rk[h] for h in heads]
    s_c = [mm(u_ind[h], sel(h, b_h), TN) + s_vk[h] for h in heads]
    for h in heads:
        tg_ref[:, h * 2 * t:h * 2 * t + t] = tinv[h].astype(BF16)
        tg_ref[:, h * 2 * t + t:(h + 1) * 2 * t] = g_mat[h].astype(BF16)
    for p in range(2):
        sl = slice(p * LANES, (p + 1) * LANES)
        uind_ref[:, sl] = u_ind[2 * p] + u_ind[2 * p + 1]
        yind_ref[:, sl] = y_ind[2 * p] + y_ind[2 * p + 1]
        sconst_ref[:, sl] = s_c[2 * p] + s_c[2 * p + 1]


def _rwkv_prep(proj, prm, bsz):
    m = proj.shape[0]
    nc = m // bsz // CHUNK
    sub = CHUNK // 8
    col = lambda cb: pl.BlockSpec((CHUNK, GROUP_W), lambda i, cb=cb: (i, cb))
    prev = lambda cb: pl.BlockSpec((8, GROUP_W), lambda i, cb=cb: (jnp.maximum(i * sub - 1, 0), cb))
    full = lambda a: pl.BlockSpec(a.shape, lambda i, n=a.ndim: (0,) * n)
    wide = pl.BlockSpec((CHUNK, GROUP_W), lambda i: (i, 0))
    params = [prm["mu_r"], prm["mu_k"], prm["mu_v"], prm["mu_z"], prm["mu_s"], prm["w0"], prm["w2p"],
              prm["a0"], prm["a2p"], prm["k_k"], prm["k_a"], prm["r_k"]]
    wide_bf = jax.ShapeDtypeStruct((m, GROUP_W), BF16)
    wide_f = jax.ShapeDtypeStruct((m, GROUP_W), F32)
    return pl.pallas_call(
        functools.partial(_rwkv_prep_kernel, nc=nc),
        grid=(m // CHUNK,),
        in_specs=[col(CB["d_r"]), col(CB["d_k"]), col(CB["d_v"]), col(CB["d_z"]),
                  pl.BlockSpec((CHUNK, LANES), lambda i: (i, SMALL_CB)),
                  prev(CB["d_r"]), prev(CB["d_k"]), prev(CB["d_v"]), prev(CB["d_z"]),
                  pl.BlockSpec((8, LANES), lambda i: (jnp.maximum(i * sub - 1, 0), SMALL_CB))]
        + [full(a) for a in params],
        out_specs=[wide, wide, wide,
                   pl.BlockSpec((CHUNK, 4 * GROUP_W), lambda i: (i, 0)),
                   wide, wide, wide, wide, wide,
                   pl.BlockSpec((1, 8, GROUP_W), lambda i: (i, 0, 0))],
        out_shape=[wide_bf, wide_bf, wide_bf, jax.ShapeDtypeStruct((m, 4 * GROUP_W), BF16),
                   wide_f, wide_f, wide_f, wide_f, wide_f,
                   jax.ShapeDtypeStruct((m // CHUNK, 8, GROUP_W), F32)],
        compiler_params=_cparams(("parallel",)),
        name="rwkv_prep",
    )(proj, proj, proj, proj, proj, proj, proj, proj, proj, proj, *params)


def _rwkv_scan_kernel(als_ref, rs_ref, bh_ref, tg_ref, yind_ref, bonus_ref, gate_ref, sconst_ref, wc_ref,
                      g_ref, out_ref, s_scr):
    t = CHUNK

    @pl.when(pl.program_id(1) == 0)
    def _():
        s_scr[...] = jnp.zeros_like(s_scr)

    lo = _lo_mask((t, LANES))
    blockdiag = (_iota((LANES, LANES), 0) < HEAD_DIM) == (_iota((LANES, LANES), 1) < HEAD_DIM)
    zero = jnp.zeros((), F32)
    heads = range(GROUP_HEADS)
    sls = [slice(p * LANES, (p + 1) * LANES) for p in range(2)]
    masks = [lo if h % 2 == 0 else ~lo for h in heads]
    s_ps = [s_scr[p] for p in range(2)]
    x = [_mm_x(jnp.concatenate([als_ref[:, sls[p]], rs_ref[:, sls[p]]], axis=0), s_ps[p], NT, pb=PASSES_STATE)
         for p in range(2)]
    tg = [jnp.concatenate([tg_ref[:, h * 2 * t:h * 2 * t + t], tg_ref[:, h * 2 * t + t:(h + 1) * 2 * t]], axis=0)
          for h in heads]
    res = [_mm_x(tg[h], jnp.where(masks[h], x[h // 2][:t], zero), pb=PASSES_STATE) for h in heads]
    tx = [res[2 * p][:t] + res[2 * p + 1][:t] for p in range(2)]
    gx = [res[2 * p][t:] + res[2 * p + 1][t:] for p in range(2)]
    upd = [_mm_x(tx[p], bh_ref[:, sls[p]], TN, pa=PASSES_STATE) for p in range(2)]
    for p in range(2):
        s_scr[p] = s_ps[p] * wc_ref[0][0:1, sls[p]] + sconst_ref[:, sls[p]] + jnp.where(blockdiag, upd[p], zero)
    y_tiles = [x[p][t:] + gx[p] + yind_ref[:, sls[p]] for p in range(2)]
    y = _head_rms(jnp.concatenate(y_tiles, axis=1), g_ref[...]) + bonus_ref[...]
    out_ref[...] = (y * gate_ref[...]).astype(out_ref.dtype)


def _rwkv_scan(prep, g, bsz):
    als, rs, bh, tg, _uind, yind, bonus, gate, sconst, wc = prep
    m = als.shape[0]
    nc = m // bsz // CHUNK
    wide = pl.BlockSpec((CHUNK, GROUP_W), lambda b, c: (b * nc + c, 0))
    return pl.pallas_call(
        _rwkv_scan_kernel,
        grid=(bsz, nc),
        in_specs=[wide, wide, wide,
                  pl.BlockSpec((CHUNK, 4 * GROUP_W), lambda b, c: (b * nc + c, 0)),
                  wide, wide, wide, wide,
                  pl.BlockSpec((1, 8, GROUP_W), lambda b, c: (b * nc + c, 0, 0)),
                  pl.BlockSpec(g.shape, lambda b, c: (0, 0))],
        out_specs=wide,
        out_shape=jax.ShapeDtypeStruct((m, GROUP_W), BF16),
        scratch_shapes=[pltpu.VMEM((2, LANES, LANES), F32)],
        compiler_params=_cparams(("parallel", "arbitrary")),
        name="rwkv_scan",
    )(als, rs, bh, tg, yind, bonus, gate, sconst, wc, g)


def _row(v, width=None, offset=0):
    v = v.astype(F32).reshape(-1)
    width = v.shape[0] if width is None else width
    return jnp.zeros((1, width), F32).at[0, offset:offset + v.shape[0]].set(v)


def _w_in_segments():
    gw = GROUP_W
    b0 = 3 * gw
    c0 = b0 + 4 * gw + GROUP_HEADS
    d0 = c0 + 5 * gw + 2 * GROUP_HEADS
    m0 = d0 + 4 * gw + DECAY_RANK + A_RANK
    wide = [0, gw, 2 * gw,
            b0, b0 + gw, b0 + 2 * gw, b0 + 3 * gw + GROUP_HEADS,
            c0, c0 + gw, c0 + 2 * gw, c0 + 3 * gw + 2 * GROUP_HEADS, c0 + 4 * gw + 2 * GROUP_HEADS,
            d0, d0 + gw + DECAY_RANK, d0 + 2 * gw + DECAY_RANK, d0 + 3 * gw + DECAY_RANK + A_RANK,
            m0, m0 + gw]
    narrow = [(b0 + 3 * gw, GROUP_HEADS, SM_BF), (c0 + 3 * gw, 2 * GROUP_HEADS, SM_CI),
              (d0 + gw, DECAY_RANK, SM_WLO), (d0 + 3 * gw + DECAY_RANK, A_RANK, SM_ALO)]
    return wide, narrow, m0 + 2 * gw


def _wlayout_kernel(w_ref, o_ref):
    wide, narrow, _ = _w_in_segments()
    rows = w_ref.shape[0]
    for seg, src in enumerate(wide):
        for half in range(GROUP_W // LANES):
            s = src + half * LANES
            base, off = (s // LANES) * LANES, s % LANES
            win = w_ref[:, base:base + 2 * LANES]
            blk = win[:, :LANES] if off == 0 else pltpu.roll(win, 2 * LANES - off, axis=1)[:, :LANES]
            d = seg * GROUP_W + half * LANES
            o_ref[:, d:d + LANES] = blk.astype(o_ref.dtype)
    lane = _iota((rows, LANES), 1)
    small = jnp.zeros((rows, LANES), F32)
    for src, width, dst in narrow:
        base, off = (src // LANES) * LANES, src % LANES
        win = w_ref[:, base:base + LANES]
        if dst != off:
            win = pltpu.roll(win, (dst - off) % LANES, axis=1)
        small = jnp.where((lane >= dst) & (lane < dst + width), win, small)
    o_ref[:, N_WIDE:] = small.astype(o_ref.dtype)


def _layout_w_in(w_in, rows=256):
    d, n_src = w_in.shape
    assert n_src == _w_in_segments()[2]
    w_pad = jnp.pad(w_in, ((0, 0), (0, NP + LANES - n_src)))
    return pl.pallas_call(
        _wlayout_kernel,
        grid=(d // rows,),
        in_specs=[pl.BlockSpec((rows, NP + LANES), lambda i: (i, 0))],
        out_specs=pl.BlockSpec((rows, NP), lambda i: (i, 0)),
        out_shape=jax.ShapeDtypeStruct((d, NP), BF16),
        compiler_params=_cparams(("parallel",)),
        name="wlayout",
    )(w_pad)


def _layout_mu(mu):
    gw = GROUP_W
    o_w, o_k, o_v, o_a, o_z = gw, gw + DECAY_RANK, 2 * gw + DECAY_RANK, 3 * gw + DECAY_RANK, 3 * gw + DECAY_RANK + A_RANK
    mu_s = jnp.zeros((1, LANES), F32)
    mu_s = mu_s.at[0, SM_WLO:SM_WLO + DECAY_RANK].set(mu[o_w:o_w + DECAY_RANK])
    mu_s = mu_s.at[0, SM_ALO:SM_ALO + A_RANK].set(mu[o_a:o_a + A_RANK])
    return dict(mu_r=_row(mu[0:gw]), mu_k=_row(mu[o_k:o_k + gw]), mu_v=_row(mu[o_v:o_v + gw]),
                mu_z=_row(mu[o_z:o_z + gw]), mu_s=mu_s)


def _pad_rows(w, offset):
    return jnp.zeros((LANES, w.shape[1]), F32).at[offset:offset + w.shape[0]].set(w).astype(BF16)


def _layer(x2, mem2, bsz, n_mem, norm_g, w_in, w_out, sgu_norm_g, sgu_w, sgu_b, fox_q_g, fox_k_g, fox_f_b,
           mlstm_conv_w, mlstm_conv_b, mlstm_i_b, mlstm_f_b, mlstm_out_g,
           rwkv_mu, rwkv_w0, rwkv_w2, rwkv_a0, rwkv_a2, rwkv_k_k, rwkv_k_a, rwkv_r_k, rwkv_ln_g,
           mem_norm_g, mem_w_kv, mem_q_g, mem_k_g):
    tile_h = lambda g: _row(jnp.tile(g, GROUP_HEADS))
    proj = _inproj(x2, _row(norm_g), _layout_w_in(w_in))

    ya = _sgu(proj, _row(sgu_norm_g), sgu_w, jnp.repeat(sgu_b.T, HEAD_DIM, axis=1))

    kaug, qaug_t, v_t = _foxprep(proj, tile_h(fox_q_g), tile_h(fox_k_g), _row(fox_f_b, LANES, SM_BF), bsz)
    yb = _fox(proj, kaug, qaug_t, v_t, bsz)

    cw = jnp.zeros((8, 2 * GROUP_W), F32).at[:CONV_K].set(mlstm_conv_w)
    yc = _mlstm(proj, cw, _row(mlstm_conv_b), _row(mlstm_i_b, LANES, SM_CI), _row(mlstm_f_b, LANES, SM_CF),
                _row(mlstm_out_g), bsz)

    prm = _layout_mu(rwkv_mu)
    prm.update(w0=_row(rwkv_w0), w2p=_pad_rows(rwkv_w2, SM_WLO), a0=_row(rwkv_a0), a2p=_pad_rows(rwkv_a2, SM_ALO),
               k_k=_row(rwkv_k_k), k_a=_row(rwkv_k_a), r_k=_row(rwkv_r_k))
    yd = _rwkv_scan(_rwkv_prep(proj, prm, bsz), _row(rwkv_ln_g), bsz)

    mk, mv = _memkv(mem2, _row(mem_norm_g), mem_w_kv.astype(BF16), tile_h(mem_k_g), n_mem)
    ym = _memattn(proj, mk, mv, tile_h(mem_q_g), bsz, n_mem)

    return _outproj([ya, yb, yc, yd, ym], w_out.astype(BF16), x2)


def kernel(x, mem, norm_g, w_in, w_out, sgu_norm_g, sgu_w, sgu_b, fox_q_g, fox_k_g, fox_f_b, mlstm_conv_w, mlstm_conv_b, mlstm_i_b, mlstm_f_b, mlstm_out_g, rwkv_mu, rwkv_w0, rwkv_w2, rwkv_a0, rwkv_a2, rwkv_k_k, rwkv_k_a, rwkv_r_k, rwkv_ln_g, mem_norm_g, mem_w_kv, mem_q_g, mem_k_g):
    bsz, seq, d = x.shape
    n_mem = mem.shape[1]
    stacked = (norm_g, w_in, w_out, sgu_norm_g, sgu_w, sgu_b, fox_q_g, fox_k_g, fox_f_b,
               mlstm_conv_w, mlstm_conv_b, mlstm_i_b, mlstm_f_b, mlstm_out_g,
               rwkv_mu, rwkv_w0, rwkv_w2, rwkv_a0, rwkv_a2, rwkv_k_k, rwkv_k_a, rwkv_r_k, rwkv_ln_g,
               mem_norm_g, mem_w_kv, mem_q_g, mem_k_g)
    x2 = x.reshape(bsz * seq, d)
    mem2 = mem.reshape(bsz * n_mem, d)
    for layer in range(norm_g.shape[0]):
        x2 = _layer(x2, mem2, bsz, n_mem, *[p[layer] for p in stacked])
    return x2.reshape(bsz, seq, d)
```

```python
import functools

import jax
import jax.numpy as jnp
from jax import lax
from jax.experimental import pallas as pl
from jax.experimental.pallas import tpu as pltpu

F32 = jnp.float32
BF16 = jnp.bfloat16

HEAD_DIM = 64
GROUP_HEADS = 4
GROUP_W = GROUP_HEADS * HEAD_DIM
LANES = 128
CHUNK = 128
CONV_K = 4
EPS = 1e-6
SCALE = HEAD_DIM ** -0.5

CB = dict(a_u=0, a_v=1, a_z=2, b_q=3, b_k=4, b_v=5, b_z=6, c_q=7, c_k=8, c_v=9, c_o=10, c_z=11,
          d_r=12, d_k=13, d_v=14, d_z=15, m_q=16, m_z=17)
N_WIDE = 18 * GROUP_W
SMALL_CB = N_WIDE // LANES
NP = N_WIDE + LANES
SM_BF, SM_CI, SM_CF, SM_WLO, SM_ALO = 0, 4, 8, 16, 32
DECAY_RANK = 16
A_RANK = 16

VMEM_LIMIT = 48 * 1024 * 1024

NN = (((1,), (0,)), ((), ()))
NT = (((1,), (1,)), ((), ()))
TN = (((0,), (0,)), ((), ()))


def _mm(a, b, dims=NN):
    return lax.dot_general(a.astype(BF16), b.astype(BF16), dims, preferred_element_type=F32)


def _split(x, n):
    if x.dtype == BF16 or n == 1:
        return [x.astype(BF16)]
    parts, r = [], x
    for _ in range(n):
        p = r.astype(BF16)
        parts.append(p)
        r = r - p.astype(F32)
    return parts


def _mm_x(a, b, dims=NN, pa=1, pb=1):
    pa_list, pb_list = _split(a, pa), _split(b, pb)
    order = max(len(pa_list), len(pb_list))
    acc = None
    for i, ai in enumerate(pa_list):
        for j, bj in enumerate(pb_list):
            if i + j < order:
                t = lax.dot_general(ai, bj, dims, preferred_element_type=F32)
                acc = t if acc is None else acc + t
    return acc


def _iota(shape, axis):
    return lax.broadcasted_iota(jnp.int32, shape, axis)


def _tril(n, strict=False):
    r, c = _iota((n, n), 0), _iota((n, n), 1)
    return (c < r) if strict else (c <= r)


def _head_sum(x):
    w = x.shape[-1]
    bd = (_iota((w, w), 0) // HEAD_DIM == _iota((w, w), 1) // HEAD_DIM).astype(BF16)
    return _mm_x(x, bd, pa=3)


def _head_rms(x, g):
    ms = _head_sum(x * x) * (1.0 / HEAD_DIM)
    return x * lax.rsqrt(ms + EPS) * g


def _lo_mask(shape):
    return _iota(shape, len(shape) - 1) % LANES < HEAD_DIM


def _cparams(sem):
    return pltpu.CompilerParams(dimension_semantics=sem, vmem_limit_bytes=VMEM_LIMIT)


def _inproj_kernel(x_ref, g_ref, w_ref, o_ref, *, col_chunk):
    x = x_ref[...]
    ms = jnp.mean(x * x, axis=-1, keepdims=True)
    h = (x * lax.rsqrt(ms + EPS) * g_ref[...]).astype(BF16)
    n = o_ref.shape[1]
    for c0 in range(0, n, col_chunk):
        c1 = min(c0 + col_chunk, n)
        o_ref[:, c0:c1] = jnp.dot(h, w_ref[:, c0:c1], preferred_element_type=F32)


def _inproj(x2, g, w_bf16, tm=256):
    m, d = x2.shape
    n = w_bf16.shape[1]
    return pl.pallas_call(
        functools.partial(_inproj_kernel, col_chunk=512),
        grid=(m // tm,),
        in_specs=[pl.BlockSpec((tm, d), lambda i: (i, 0)),
                  pl.BlockSpec((1, d), lambda i: (0, 0)),
                  pl.BlockSpec((d, n), lambda i: (0, 0))],
        out_specs=pl.BlockSpec((tm, n), lambda i: (i, 0)),
        out_shape=jax.ShapeDtypeStruct((m, n), F32),
        compiler_params=_cparams(("parallel",)),
        name="inproj",
    )(x2, g, w_bf16)


def _outproj_kernel(ya, yb, yc, yd, ym, w_ref, x_ref, o_ref):
    mixed = jnp.concatenate([ya[...], yb[...], yc[...], yd[...], ym[...]], axis=1)
    o_ref[...] = x_ref[...] + jnp.dot(mixed, w_ref[...], preferred_element_type=F32)


def _outproj(ys, w_bf16, x2, tm=512):
    m, d = x2.shape
    yspec = pl.BlockSpec((tm, GROUP_W), lambda i: (i, 0))
    return pl.pallas_call(
        _outproj_kernel,
        grid=(m // tm,),
        in_specs=[yspec] * 5 + [pl.BlockSpec(w_bf16.shape, lambda i: (0, 0)),
                                pl.BlockSpec((tm, d), lambda i: (i, 0))],
        out_specs=pl.BlockSpec((tm, d), lambda i: (i, 0)),
        out_shape=jax.ShapeDtypeStruct((m, d), F32),
        compiler_params=_cparams(("parallel",)),
        name="outproj",
    )(*ys, w_bf16, x2)


def _sgu_kernel(u_ref, v_ref, z_ref, g_ref, w_ref, bias_ref, o_ref):
    t = u_ref.shape[0]
    u = jax.nn.gelu(u_ref[...])
    vn = _head_rms(jax.nn.gelu(v_ref[...]), g_ref[...])
    gate = jax.nn.silu(z_ref[...])
    tril = _tril(CHUNK)
    lo = _lo_mask((CHUNK, LANES))
    ws = [jnp.where(tril, w_ref[g], 0.0).astype(BF16) for g in range(GROUP_HEADS)]
    bias = bias_ref[...]
    for c in range(t // CHUNK):
        sl = slice(c * CHUNK, (c + 1) * CHUNK)
        tiles = []
        for p in range(2):
            vt = vn[sl, p * LANES:(p + 1) * LANES]
            tiles.append(_mm(ws[2 * p], jnp.where(lo, vt, 0.0)) + _mm(ws[2 * p + 1], jnp.where(lo, 0.0, vt)))
        mixed = jnp.concatenate(tiles, axis=1) + bias
        o_ref[sl, :] = (u[sl] * mixed * gate[sl]).astype(o_ref.dtype)


def _sgu(proj, g, w, bias_tile, t=512):
    m = proj.shape[0]
    col = lambda cb: pl.BlockSpec((t, GROUP_W), lambda i, cb=cb: (i, cb))
    full = lambda a: pl.BlockSpec(a.shape, lambda i, n=a.ndim: (0,) * n)
    return pl.pallas_call(
        _sgu_kernel,
        grid=(m // t,),
        in_specs=[col(CB["a_u"]), col(CB["a_v"]), col(CB["a_z"]), full(g), full(w), full(bias_tile)],
        out_specs=pl.BlockSpec((t, GROUP_W), lambda i: (i, 0)),
        out_shape=jax.ShapeDtypeStruct((m, GROUP_W), BF16),
        compiler_params=_cparams(("parallel",)),
        name="sgu",
    )(proj, proj, proj, g, w, bias_tile)


def _memkv_kernel(mem_ref, g_ref, w_ref, kg_ref, k_ref, v_ref):
    x = mem_ref[...]
    ms = jnp.mean(x * x, axis=-1, keepdims=True)
    h = (x * lax.rsqrt(ms + EPS) * g_ref[...]).astype(BF16)
    kv = jnp.dot(h, w_ref[...], preferred_element_type=F32)
    k_ref[...] = _head_rms(kv[:, :GROUP_W], kg_ref[...]).astype(BF16)
    v_ref[...] = kv[:, GROUP_W:].astype(BF16)


def _memkv(mem2, g, w_bf16, kg, n_mem):
    m, d = mem2.shape
    full = lambda a: pl.BlockSpec(a.shape, lambda i, n=a.ndim: (0,) * n)
    out = pl.BlockSpec((n_mem, GROUP_W), lambda i: (i, 0))
    return pl.pallas_call(
        _memkv_kernel,
        grid=(m // n_mem,),
        in_specs=[pl.BlockSpec((n_mem, d), lambda i: (i, 0)), full(g), full(w_bf16), full(kg)],
        out_specs=[out, out],
        out_shape=[jax.ShapeDtypeStruct((m, GROUP_W), BF16)] * 2,
        compiler_params=_cparams(("parallel",)),
        name="memkv",
    )(mem2, g, w_bf16, kg)


def _memattn_kernel(q_ref, z_ref, k_ref, v_ref, qg_ref, o_ref):
    q = (_head_rms(q_ref[...], qg_ref[...]) * SCALE).astype(BF16)
    k = k_ref[...]
    v = v_ref[...]
    t = q.shape[0]
    lo_q = _lo_mask((t, LANES))
    lo_v = _lo_mask((v.shape[0], LANES))
    zero = jnp.zeros((), BF16)
    outs = []
    for p in range(2):
        sl = slice(p * LANES, (p + 1) * LANES)
        qp, kp, vp = q[:, sl], k[:, sl], v[:, sl]
        acc = None
        for hh in range(2):
            mq = lo_q if hh == 0 else ~lo_q
            mv = lo_v if hh == 0 else ~lo_v
            s = _mm(jnp.where(mq, qp, zero), kp, NT)
            e = jnp.exp(s - jnp.max(s, axis=-1, keepdims=True))
            pr = e / jnp.sum(e, axis=-1, keepdims=True)
            pv = _mm(pr, jnp.where(mv, vp, zero))
            acc = pv if acc is None else acc + pv
        outs.append(acc)
    y = jnp.concatenate(outs, axis=1)
    o_ref[...] = (y * jax.nn.silu(z_ref[...])).astype(o_ref.dtype)


def _memattn(proj, kn, vb, qg, bsz, n_mem, t=512):
    m = proj.shape[0]
    nt = m // bsz // t
    col = lambda cb: pl.BlockSpec((t, GROUP_W), lambda b, i, cb=cb: (b * nt + i, cb))
    kv = pl.BlockSpec((n_mem, GROUP_W), lambda b, i: (b, 0))
    return pl.pallas_call(
        _memattn_kernel,
        grid=(bsz, nt),
        in_specs=[col(CB["m_q"]), col(CB["m_z"]), kv, kv, pl.BlockSpec(qg.shape, lambda b, i: (0, 0))],
        out_specs=pl.BlockSpec((t, GROUP_W), lambda b, i: (b * nt + i, 0)),
        out_shape=jax.ShapeDtypeStruct((m, GROUP_W), BF16),
        compiler_params=_cparams(("parallel", "parallel")),
        name="memattn",
    )(proj, proj, kn, vb, qg)


LOG2E = 1.4426950408889634
F_TERMS = 3


def _bf16_terms(x, n):
    terms, r = [], x
    for _ in range(n):
        p = r.astype(BF16).astype(F32)
        terms.append(p)
        r = r - p
    return terms


def _foxprep_kernel(q_ref, k_ref, v_ref, sm_ref, qg_ref, kg_ref, fb_ref,
                    kaug_ref, qaug_t_ref, v_t_ref, carry_ref):
    @pl.when(pl.program_id(1) == 0)
    def _():
        carry_ref[...] = jnp.zeros_like(carry_ref)

    t = q_ref.shape[0]
    qn = _head_rms(q_ref[...], qg_ref[...]) * (SCALE * LOG2E)
    kn = _head_rms(k_ref[...], kg_ref[...])
    lf = jax.nn.log_sigmoid(sm_ref[...] + fb_ref[...])
    cs = _mm_x(_tril(t).astype(BF16), lf, pb=3) + carry_ref[0:1, :]
    carry_ref[...] = jnp.broadcast_to(cs[t - 1:t, :], carry_ref.shape)
    f2 = cs * LOG2E

    lane = _iota((t, LANES), 1)
    q_tiles = []
    for h in range(GROUP_HEADS):
        sl = slice((h // 2) * LANES, (h // 2 + 1) * LANES)
        q_t, k_t = qn[:, sl], kn[:, sl]
        if h % 2 == 1:
            q_t = pltpu.roll(q_t, HEAD_DIM, axis=1)
            k_t = pltpu.roll(k_t, HEAD_DIM, axis=1)
        terms = _bf16_terms(f2[:, SM_BF + h:SM_BF + h + 1], F_TERMS)
        qa = jnp.where(lane < HEAD_DIM + 2 * F_TERMS, 1.0, 0.0)
        ka = qa
        for n, term in enumerate(terms):
            qa = jnp.where(lane == HEAD_DIM + n, term, qa)
            ka = jnp.where(lane == HEAD_DIM + F_TERMS + n, -term, ka)
        qa = jnp.where(lane < HEAD_DIM, q_t, qa)
        ka = jnp.where(lane < HEAD_DIM, k_t, ka)
        kaug_ref[:, h * LANES:(h + 1) * LANES] = ka.astype(BF16)
        q_tiles.append(qa)
    qaug_t_ref[0] = jnp.concatenate(q_tiles, axis=1).T.astype(BF16)
    v_t_ref[0] = v_ref[...].T.astype(BF16)


def _foxprep(proj, qg, kg, fb_row, bsz, t=512):
    m = proj.shape[0]
    seq = m // bsz
    nt = seq // t
    col = lambda cb: pl.BlockSpec((t, GROUP_W), lambda b, i, cb=cb: (b * nt + i, cb))
    row = lambda a: pl.BlockSpec(a.shape, lambda b, i: (0, 0))
    aug_w = GROUP_HEADS * LANES
    return pl.pallas_call(
        _foxprep_kernel,
        grid=(bsz, nt),
        in_specs=[col(CB["b_q"]), col(CB["b_k"]), col(CB["b_v"]),
                  pl.BlockSpec((t, LANES), lambda b, i: (b * nt + i, SMALL_CB)),
                  row(qg), row(kg), row(fb_row)],
        out_specs=[pl.BlockSpec((t, aug_w), lambda b, i: (b * nt + i, 0)),
                   pl.BlockSpec((1, aug_w, t), lambda b, i: (b, 0, i)),
                   pl.BlockSpec((1, GROUP_W, t), lambda b, i: (b, 0, i))],
        out_shape=[jax.ShapeDtypeStruct((m, aug_w), BF16),
                   jax.ShapeDtypeStruct((bsz, aug_w, seq), BF16),
                   jax.ShapeDtypeStruct((bsz, GROUP_W, seq), BF16)],
        scratch_shapes=[pltpu.VMEM((8, LANES), F32)],
        compiler_params=_cparams(("parallel", "arbitrary")),
        name="foxprep",
    )(proj, proj, proj, proj, qg, kg, fb_row)


def _fox_kernel(q_ref, k_ref, v_ref, z_ref, o_ref, m_scr, l_scr, acc_scr, sa_scr, sb_scr, mxa_scr, mxb_scr,
                *, tq, tk):
    i = pl.program_id(2)
    m_scr[...] = jnp.full_like(m_scr, -jnp.inf)
    l_scr[...] = jnp.zeros_like(l_scr)
    acc_scr[...] = jnp.zeros_like(acc_scr)

    def scores(j, s_scr, mx_scr):
        k0 = pl.multiple_of(j * tk, tk)
        for hh in range(2):
            k_blk = k_ref[pl.ds(k0, tk), hh * LANES:(hh + 1) * LANES]
            s_t = jnp.dot(k_blk, q_ref[0, hh * LANES:(hh + 1) * LANES, :],
                          preferred_element_type=F32)
            s_scr[hh] = s_t
            mx_scr[hh] = jnp.broadcast_to(jnp.max(s_t, axis=0, keepdims=True), (8, tq))

    def softmax_pv(j, s_scr, mx_scr, masked=False):
        k0 = pl.multiple_of(j * tk, tk)
        for hh in range(2):
            s_t = s_scr[hh]
            mx = mx_scr[hh][0:1, :]
            if masked:
                visible = (k0 + _iota((tk, tq), 0)) <= (i * tq + _iota((tk, tq), 1))
                s_t = jnp.where(visible, s_t, -jnp.inf)
                mx = jnp.max(s_t, axis=0, keepdims=True)
            m_old = m_scr[hh][0:1, :]
            m_new = jnp.maximum(m_old, mx)
            alpha = jnp.exp2(m_old - m_new)
            p_t = jnp.exp2(s_t - m_new)
            l_new = alpha * l_scr[hh][0:1, :] + jnp.sum(p_t, axis=0, keepdims=True)
            m_scr[hh] = jnp.broadcast_to(m_new, (8, tq))
            l_scr[hh] = jnp.broadcast_to(l_new, (8, tq))
            v_blk = v_ref[0, hh * HEAD_DIM:(hh + 1) * HEAD_DIM, pl.ds(k0, tk)]
            acc_scr[hh] = acc_scr[hh] * alpha + jnp.dot(v_blk, p_t.astype(BF16), preferred_element_type=F32)

    assert tq == 2 * tk
    n_full = 2 * i
    scores(0, sa_scr, mxa_scr)

    def two_blocks(jj, carry):
        scores(2 * jj + 1, sb_scr, mxb_scr)
        softmax_pv(2 * jj, sa_scr, mxa_scr)
        scores(2 * jj + 2, sa_scr, mxa_scr)
        softmax_pv(2 * jj + 1, sb_scr, mxb_scr)
        return carry

    lax.fori_loop(0, i, two_blocks, 0)
    scores(n_full + 1, sb_scr, mxb_scr)
    softmax_pv(n_full, sa_scr, mxa_scr, masked=True)
    softmax_pv(n_full + 1, sb_scr, mxb_scr, masked=True)

    y_t = jnp.concatenate([acc_scr[hh] / l_scr[hh][0:1, :] for hh in range(2)], axis=0)
    o_ref[...] = (y_t.T * jax.nn.silu(z_ref[...])).astype(o_ref.dtype)


def _fox(proj, kaug, qaug_t, v_t, bsz, tq=512, tk=256):
    m = proj.shape[0]
    seq = m // bsz
    nq = seq // tq
    s_buf = pltpu.VMEM((2, tk, tq), F32)
    mx_buf = pltpu.VMEM((2, 8, tq), F32)
    return pl.pallas_call(
        functools.partial(_fox_kernel, tq=tq, tk=tk),
        grid=(bsz, 2, nq),
        in_specs=[pl.BlockSpec((1, 2 * LANES, tq), lambda b, p, i: (b, p, i)),
                  pl.BlockSpec((seq, 2 * LANES), lambda b, p, i: (b, p)),
                  pl.BlockSpec((1, LANES, seq), lambda b, p, i: (b, p, 0)),
                  pl.BlockSpec((tq, LANES), lambda b, p, i: (b * nq + i, 2 * CB["b_z"] + p))],
        out_specs=pl.BlockSpec((tq, LANES), lambda b, p, i: (b * nq + i, p)),
        out_shape=jax.ShapeDtypeStruct((m, GROUP_W), BF16),
        scratch_shapes=[pltpu.VMEM((2, 8, tq), F32), pltpu.VMEM((2, 8, tq), F32),
                        pltpu.VMEM((2, HEAD_DIM, tq), F32), s_buf, s_buf, mx_buf, mx_buf],
        compiler_params=_cparams(("parallel", "parallel", "arbitrary")),
        name="fox",
    )(qaug_t, kaug, v_t, proj)


def _mlstm_kernel(q_ref, k_ref, v_ref, og_ref, z_ref, sm_ref, cw_ref, cb_ref, ib_ref, fb_ref, g_ref,
                  out_ref, xbuf, c_scr, n_scr, m_scr):
    t = CHUNK
    nb = q_ref.shape[0]

    @pl.when(pl.program_id(1) == 0)
    def _():
        xbuf[:, 0:8, :] = jnp.zeros((nb, 8, 2 * GROUP_W), F32)
        c_scr[...] = jnp.zeros_like(c_scr)
        n_scr[...] = jnp.zeros_like(n_scr)
        m_scr[...] = jnp.zeros_like(m_scr)

    tril = _tril(t)
    tril_bf = tril.astype(BF16)
    lo = _lo_mask((t, LANES))
    lo_row = _lo_mask((1, LANES))
    blockdiag = (_iota((LANES, LANES), 0) < HEAD_DIM) == (_iota((LANES, LANES), 1) < HEAD_DIM)
    lane_row = _iota((1, LANES), 1)
    sls = [slice(p * LANES, (p + 1) * LANES) for p in range(2)]

    qs, ks, vs, lis, bcums, b_ts, i_ts, m_rows = [], [], [], [], [], [], [], []
    for b in range(nb):
        xbuf[b, 8:8 + t, 0:GROUP_W] = q_ref[b]
        xbuf[b, 8:8 + t, GROUP_W:] = k_ref[b]
        conv = cb_ref[...]
        for jj in range(CONV_K):
            conv = conv + cw_ref[jj:jj + 1, :] * xbuf[b, 8 - (CONV_K - 1) + jj:8 - (CONV_K - 1) + jj + t, :]
        xbuf[b, 0:8, :] = xbuf[b, t:t + 8, :]
        qk_act = jax.nn.silu(conv)
        qs.append(qk_act[:, :GROUP_W])
        ks.append(qk_act[:, GROUP_W:] * SCALE)
        vs.append(v_ref[b])
        sm = sm_ref[b]
        li = sm + ib_ref[...]
        lf = jax.nn.log_sigmoid(sm + fb_ref[...])
        bcum = _mm_x(tril_bf, lf, pb=3)
        lis.append(li)
        bcums.append(bcum)
        b_ts.append(bcum.T)
        i_ts.append(li.T)
        m_rows.append(m_scr[b, 0:1, :])

    units = [(b, h) for b in range(nb) for h in range(GROUP_HEADS)]
    pairs = [(b, p) for b in range(nb) for p in range(2)]
    idx = range(len(units))
    pair_of = [2 * b + h // 2 for b, h in units]
    masks = [lo if h % 2 == 0 else ~lo for _, h in units]
    qp = [qs[b][:, sls[p]] for b, p in pairs]
    kp = [ks[b][:, sls[p]] for b, p in pairs]
    vp = [vs[b][:, sls[p]] for b, p in pairs]
    c_ps = [c_scr[b, p] for b, p in pairs]
    n_ps = [n_scr[b, p][0:1, :] for b, p in pairs]
    qk = [_mm(jnp.where(masks[u], qp[pair_of[u]], 0.0), kp[pair_of[u]], NT) for u in idx]
    q_c = [_mm(qp[i], c_ps[i]) for i in range(len(pairs))]
    q_n = [qp[i] * n_ps[i] for i in range(len(pairs))]
    b_c = [bcums[b][:, SM_CF + h:SM_CF + h + 1] for b, h in units]
    b_r = [b_ts[b][SM_CF + h:SM_CF + h + 1, :] for b, h in units]
    i_c = [lis[b][:, SM_CI + h:SM_CI + h + 1] for b, h in units]
    i_r = [i_ts[b][SM_CI + h:SM_CI + h + 1, :] for b, h in units]
    m_st = [m_rows[b][:, h:h + 1] for b, h in units]
    log_d = [jnp.where(tril, b_c[u] - b_r[u] + i_r[u], -jnp.inf) for u in idx]
    inter = [b_c[u] + m_st[u] for u in idx]
    m_t = [jnp.maximum(jnp.max(log_d[u], axis=-1, keepdims=True), inter[u]) for u in idx]
    s = [qk[u] * jnp.exp(log_d[u] - m_t[u]) for u in idx]
    w_inter = [jnp.exp(inter[u] - m_t[u]) for u in idx]
    sv = [_mm(s[u], jnp.where(masks[u], vp[pair_of[u]], 0.0)) for u in idx]
    num = [sv[u] + w_inter[u] * jnp.where(masks[u], q_c[pair_of[u]], 0.0) for u in idx]
    den = [jnp.sum(s[u], axis=-1, keepdims=True)
           + w_inter[u] * jnp.sum(jnp.where(masks[u], q_n[pair_of[u]], 0.0), axis=-1, keepdims=True) for u in idx]
    hv = [num[u] / jnp.maximum(jnp.abs(den[u]), jnp.exp(-m_t[u])) for u in idx]
    g = [b_c[u][t - 1:t, :] for u in idx]
    m_new = [jnp.maximum(g[u] + m_st[u], jnp.max(g[u] - b_r[u] + i_r[u], axis=-1, keepdims=True)) for u in idx]
    w_col = [jnp.exp(g[u] - b_c[u] + i_c[u] - m_new[u]) for u in idx]
    cd = [jnp.exp(g[u] + m_st[u] - m_new[u]) for u in idx]
    first = lambda b, p: GROUP_HEADS * b + 2 * p
    kw = [kp[i] * jnp.where(lo, w_col[first(b, p)], w_col[first(b, p) + 1]) for i, (b, p) in enumerate(pairs)]
    cd_row = [jnp.where(lo_row, cd[first(b, p)], cd[first(b, p) + 1]) for b, p in pairs]
    upd = [_mm(kw[i], vp[i], TN) for i in range(len(pairs))]
    for i, (b, p) in enumerate(pairs):
        c_scr[b, p] = cd_row[i] * c_ps[i] + jnp.where(blockdiag, upd[i], 0.0)
        n_new = cd_row[i] * n_ps[i] + jnp.sum(kw[i], axis=0, keepdims=True)
        n_scr[b, p] = jnp.broadcast_to(n_new, (8, LANES))
    for b in range(nb):
        m_row_new = m_rows[b]
        for h in range(GROUP_HEADS):
            m_row_new = jnp.where(lane_row == h, m_new[GROUP_HEADS * b + h], m_row_new)
        m_scr[b] = jnp.broadcast_to(m_row_new, (8, LANES))
        h_tiles = [hv[first(b, p)] + hv[first(b, p) + 1] for p in range(2)]
        hcat = jax.nn.sigmoid(og_ref[b]) * jnp.concatenate(h_tiles, axis=1)
        y = _head_rms(hcat, g_ref[...])
        out_ref[b] = (y * jax.nn.silu(z_ref[b])).astype(out_ref.dtype)


def _mlstm(proj, cw, cb, ib_row, fb_row, g, bsz, nb=4):
    m = proj.shape[0]
    seq = m // bsz
    nc = seq // CHUNK
    nb = min(nb, bsz)
    proj3 = proj.reshape(bsz, seq, proj.shape[1])
    col = lambda cb_: pl.BlockSpec((nb, CHUNK, GROUP_W), lambda b, c, cb_=cb_: (b, c, cb_))
    full = lambda a: pl.BlockSpec(a.shape, lambda b, c, n=a.ndim: (0,) * n)
    out = pl.pallas_call(
        _mlstm_kernel,
        grid=(bsz // nb, nc),
        in_specs=[col(CB["c_q"]), col(CB["c_k"]), col(CB["c_v"]), col(CB["c_o"]), col(CB["c_z"]),
                  pl.BlockSpec((nb, CHUNK, LANES), lambda b, c: (b, c, SMALL_CB)),
                  full(cw), full(cb), full(ib_row), full(fb_row), full(g)],
        out_specs=pl.BlockSpec((nb, CHUNK, GROUP_W), lambda b, c: (b, c, 0)),
        out_shape=jax.ShapeDtypeStruct((bsz, seq, GROUP_W), BF16),
        scratch_shapes=[pltpu.VMEM((nb, CHUNK + 8, 2 * GROUP_W), F32), pltpu.VMEM((nb, 2, LANES, LANES), F32),
                        pltpu.VMEM((nb, 2, 8, LANES), F32), pltpu.VMEM((nb, 8, LANES), F32)],
        compiler_params=_cparams(("parallel", "arbitrary")),
        name="mlstm",
    )(proj3, proj3, proj3, proj3, proj3, proj3, cw, cb, ib_row, fb_row, g)
    return out.reshape(m, GROUP_W)


PASSES_AA = 1
PASSES_INV = 1
PASSES_STATE = 1


def _mmp(a, b, dims=NN, passes=1):
    return _mm_x(a, b, dims, pa=passes, pb=passes)


def _tri_inverse(mats):
    n = mats[0].shape[0]
    r, c = _iota((n, n), 0), _iota((n, n), 1)
    same = lambda b: (r // b) == (c // b)
    mm = lambda x, y: _mmp(x, y, passes=PASSES_INV)
    eye = jnp.where(r == c, 1.0, 0.0)
    a8 = [jnp.where(same(8), a, 0.0) for a in mats]
    ts = [eye + x for x in a8]
    ps = [mm(x, x) for x in a8]
    ts = [t + mm(t, p) for t, p in zip(ts, ps)]
    ps = [mm(p, p) for p in ps]
    ts = [t + mm(t, p) for t, p in zip(ts, ps)]
    b = 8
    while b < n:
        off = same(2 * b) & ~same(b)
        inner = [mm(jnp.where(off, a, 0.0), t) for a, t in zip(mats, ts)]
        ts = [t + mm(t, x) for t, x in zip(ts, inner)]
        b *= 2
    return ts


def _rwkv_prep_kernel(r_ref, k_ref, v_ref, z_ref, sm_ref, pr_ref, pk_ref, pv_ref, pz_ref, psm_ref,
                      mur_ref, muk_ref, muv_ref, muz_ref, mus_ref,
                      w0_ref, w2_ref, a0_ref, a2_ref, kk_ref, ka_ref, rk_ref,
                      als_ref, rs_ref, bh_ref, tg_ref, uind_ref, yind_ref, bonus_ref, gate_ref,
                      sconst_ref, wc_ref, *, steps_per_seq):
    t = CHUNK
    first = (pl.program_id(0) % steps_per_seq) == 0
    row0 = _iota((r_ref.shape[0], 1), 0) == 0

    def shifted(x_ref, p_ref, mu_ref):
        x = x_ref[...]
        prev_row = jnp.where(first, 0.0, p_ref[7:8, :])
        prev = jnp.where(row0, prev_row, pltpu.roll(x, 1, axis=0))
        return x + mu_ref[...] * (prev - x)

    r = shifted(r_ref, pr_ref, mur_ref)
    k = shifted(k_ref, pk_ref, muk_ref)
    v = shifted(v_ref, pv_ref, muv_ref)
    z = shifted(z_ref, pz_ref, muz_ref)
    sm = shifted(sm_ref, psm_ref, mus_ref)

    w_log = -jax.nn.softplus(-(w0_ref[...] + _mm(jnp.tanh(sm), w2_ref[...]))) - 0.5
    ld = -jnp.exp(w_log)
    a = jax.nn.sigmoid(a0_ref[...] + _mm(sm, a2_ref[...]))
    kk = k * kk_ref[...]
    kk = kk / jnp.maximum(jnp.sqrt(_head_sum(kk * kk)), 1e-12)
    k2 = k * (1.0 + (a - 1.0) * ka_ref[...])
    kka = kk * a

    rows = r.shape[0]
    n_sub = rows // t
    rr, cc = _iota((rows, rows), 0), _iota((rows, rows), 1)
    chunk_tril = ((cc <= rr) & (rr // t == cc // t)).astype(BF16)
    lw = _mm_x(chunk_tril, ld, pb=3)
    lw_ex = lw - ld
    per_chunk = lambda row_of: jnp.concatenate(
        [jnp.broadcast_to(lw[c * t + row_of:c * t + row_of + 1, :], (t, GROUP_W)) for c in range(n_sub)], axis=0)
    lw_mid = per_chunk(t // 2 - 1)
    lw_end = per_chunk(t - 1)
    e_in = jnp.exp(lw - lw_mid)
    e_out = jnp.exp(lw_mid - lw)
    al_m = -kk * jnp.exp(lw_ex - lw_mid)
    r_m = r * e_in
    be_m = kka * e_out
    k_m = k2 * e_out
    al_s = -kk * jnp.exp(lw_ex)
    r_s = r * jnp.exp(lw)
    e_end = jnp.exp(lw_end - lw)
    b_h = kka * e_end
    k_h = k2 * e_end

    als_ref[...] = al_s.astype(BF16)
    rs_ref[...] = r_s.astype(BF16)
    bh_ref[...] = b_h.astype(BF16)
    bonus_ref[...] = _head_sum(r * k2 * rk_ref[...]) * v
    gate_ref[...] = jax.nn.silu(z)
    for c in range(n_sub):
        wc_ref[c] = jnp.exp(lw_end[c * t:c * t + 8, :])

    strict = _tril(t, strict=True)
    incl = _tril(t)
    lo = _lo_mask((t, LANES))
    units = [(c, h) for c in range(n_sub) for h in range(GROUP_HEADS)]
    idx = range(len(units))
    rsl = [slice(c * t, (c + 1) * t) for c, _ in units]
    csl = [slice((h // 2) * LANES, (h // 2 + 1) * LANES) for _, h in units]
    masks = [lo if h % 2 == 0 else ~lo for _, h in units]
    mm = lambda x, y, dims=NN: _mmp(x, y, dims, passes=PASSES_INV)
    sel = lambda u, x: jnp.where(masks[u], x[rsl[u], csl[u]], 0.0)
    aa = [_mmp(jnp.concatenate([sel(u, al_m), sel(u, r_m)], axis=0),
               jnp.concatenate([be_m[rsl[u], csl[u]], k_m[rsl[u], csl[u]]], axis=0), NT, passes=PASSES_AA)
          for u in idx]
    a_ab = [jnp.where(strict, x[:t, :t], 0.0) for x in aa]
    a_ak = [jnp.where(strict, x[:t, t:], 0.0) for x in aa]
    a_rb = [jnp.where(incl, x[t:, :t], 0.0) for x in aa]
    a_rk = [jnp.where(incl, x[t:, t:], 0.0) for x in aa]
    v_h = [sel(u, v) for u in idx]
    av = [mm(a_ak[u], v_h[u]) for u in idx]
    y_rk = [mm(a_rk[u], v_h[u]) for u in idx]
    s_vk = [mm(v_h[u], sel(u, k_h), TN) for u in idx]
    tinv = _tri_inverse(a_ab)
    u_ind = [mm(tinv[u], av[u]) for u in idx]
    g_mat = [mm(a_rb[u], tinv[u]) for u in idx]
    y_ind = [mm(a_rb[u], u_ind[u]) + y_rk[u] for u in idx]
    s_c = [mm(u_ind[u], sel(u, b_h), TN) + s_vk[u] for u in idx]
    for u, (c, h) in enumerate(units):
        tg_ref[rsl[u], h * 2 * t:h * 2 * t + t] = tinv[u].astype(BF16)
        tg_ref[rsl[u], h * 2 * t + t:(h + 1) * 2 * t] = g_mat[u].astype(BF16)
    for c in range(n_sub):
        for p in range(2):
            u0 = c * GROUP_HEADS + 2 * p
            rs_, sl = slice(c * t, (c + 1) * t), slice(p * LANES, (p + 1) * LANES)
            uind_ref[rs_, sl] = u_ind[u0] + u_ind[u0 + 1]
            yind_ref[rs_, sl] = y_ind[u0] + y_ind[u0 + 1]
            sconst_ref[rs_, sl] = s_c[u0] + s_c[u0 + 1]


def _rwkv_prep(proj, prm, bsz, n_sub=4):
    m = proj.shape[0]
    rows = n_sub * CHUNK
    sub = rows // 8
    col = lambda cb: pl.BlockSpec((rows, GROUP_W), lambda i, cb=cb: (i, cb))
    prev = lambda cb: pl.BlockSpec((8, GROUP_W), lambda i, cb=cb: (jnp.maximum(i * sub - 1, 0), cb))
    full = lambda a: pl.BlockSpec(a.shape, lambda i, n=a.ndim: (0,) * n)
    wide = pl.BlockSpec((rows, GROUP_W), lambda i: (i, 0))
    params = [prm["mu_r"], prm["mu_k"], prm["mu_v"], prm["mu_z"], prm["mu_s"], prm["w0"], prm["w2p"],
              prm["a0"], prm["a2p"], prm["k_k"], prm["k_a"], prm["r_k"]]
    wide_bf = jax.ShapeDtypeStruct((m, GROUP_W), BF16)
    wide_f = jax.ShapeDtypeStruct((m, GROUP_W), F32)
    return pl.pallas_call(
        functools.partial(_rwkv_prep_kernel, steps_per_seq=m // bsz // rows),
        grid=(m // rows,),
        in_specs=[col(CB["d_r"]), col(CB["d_k"]), col(CB["d_v"]), col(CB["d_z"]),
                  pl.BlockSpec((rows, LANES), lambda i: (i, SMALL_CB)),
                  prev(CB["d_r"]), prev(CB["d_k"]), prev(CB["d_v"]), prev(CB["d_z"]),
                  pl.BlockSpec((8, LANES), lambda i: (jnp.maximum(i * sub - 1, 0), SMALL_CB))]
        + [full(a) for a in params],
        out_specs=[wide, wide, wide,
                   pl.BlockSpec((rows, 4 * GROUP_W), lambda i: (i, 0)),
                   wide, wide, wide, wide, wide,
                   pl.BlockSpec((n_sub, 8, GROUP_W), lambda i: (i, 0, 0))],
        out_shape=[wide_bf, wide_bf, wide_bf, jax.ShapeDtypeStruct((m, 4 * GROUP_W), BF16),
                   wide_f, wide_f, wide_f, wide_f, wide_f,
                   jax.ShapeDtypeStruct((m // CHUNK, 8, GROUP_W), F32)],
        compiler_params=_cparams(("parallel",)),
        name="rwkv_prep",
    )(proj, proj, proj, proj, proj, proj, proj, proj, proj, proj, *params)


def _rwkv_scan_kernel(als_ref, rs_ref, bh_ref, tg_ref, yind_ref, bonus_ref, gate_ref, sconst_ref, wc_ref,
                      g_ref, out_ref, s_scr):
    t = CHUNK

    @pl.when(pl.program_id(1) == 0)
    def _():
        s_scr[...] = jnp.zeros_like(s_scr)

    nb = als_ref.shape[0]
    lo = _lo_mask((t, LANES))
    blockdiag = (_iota((LANES, LANES), 0) < HEAD_DIM) == (_iota((LANES, LANES), 1) < HEAD_DIM)
    zero = jnp.zeros((), F32)
    sls = [slice(p * LANES, (p + 1) * LANES) for p in range(2)]
    pairs = [(b, p) for b in range(nb) for p in range(2)]
    units = [(b, h) for b in range(nb) for h in range(GROUP_HEADS)]
    s_ps = [s_scr[b, p] for b, p in pairs]
    x = [_mm_x(jnp.concatenate([als_ref[b, :, sls[p]], rs_ref[b, :, sls[p]]], axis=0), s_ps[i], NT,
               pb=PASSES_STATE) for i, (b, p) in enumerate(pairs)]
    tg = [jnp.concatenate([tg_ref[b, :, h * 2 * t:h * 2 * t + t], tg_ref[b, :, h * 2 * t + t:(h + 1) * 2 * t]],
                          axis=0) for b, h in units]
    res = [_mm_x(tg[u], jnp.where(lo if h % 2 == 0 else ~lo, x[2 * b + h // 2][:t], zero), pb=PASSES_STATE)
           for u, (b, h) in enumerate(units)]
    tx = [res[GROUP_HEADS * b + 2 * p][:t] + res[GROUP_HEADS * b + 2 * p + 1][:t] for b, p in pairs]
    gx = [res[GROUP_HEADS * b + 2 * p][t:] + res[GROUP_HEADS * b + 2 * p + 1][t:] for b, p in pairs]
    upd = [_mm_x(tx[i], bh_ref[b, :, sls[p]], TN, pa=PASSES_STATE) for i, (b, p) in enumerate(pairs)]
    for i, (b, p) in enumerate(pairs):
        s_scr[b, p] = (s_ps[i] * wc_ref[b, 0][0:1, sls[p]] + sconst_ref[b, :, sls[p]]
                       + jnp.where(blockdiag, upd[i], zero))
    for b in range(nb):
        y_tiles = [x[2 * b + p][t:] + gx[2 * b + p] + yind_ref[b, :, sls[p]] for p in range(2)]
        y = _head_rms(jnp.concatenate(y_tiles, axis=1), g_ref[...]) + bonus_ref[b]
        out_ref[b] = (y * gate_ref[b]).astype(out_ref.dtype)


def _rwkv_scan(prep, g, bsz, nb=4):
    als, rs, bh, tg, _uind, yind, bonus, gate, sconst, wc = prep
    m = als.shape[0]
    nb = min(nb, bsz)
    seq = m // bsz
    nc = seq // CHUNK
    per_batch = lambda a: a.reshape((bsz, seq) + a.shape[1:])
    wide = pl.BlockSpec((nb, CHUNK, GROUP_W), lambda b, c: (b, c, 0))
    out = pl.pallas_call(
        _rwkv_scan_kernel,
        grid=(bsz // nb, nc),
        in_specs=[wide, wide, wide,
                  pl.BlockSpec((nb, CHUNK, 4 * GROUP_W), lambda b, c: (b, c, 0)),
                  wide, wide, wide, wide,
                  pl.BlockSpec((nb, 1, 8, GROUP_W), lambda b, c: (b, c, 0, 0)),
                  pl.BlockSpec(g.shape, lambda b, c: (0, 0))],
        out_specs=wide,
        out_shape=jax.ShapeDtypeStruct((bsz, seq, GROUP_W), BF16),
        scratch_shapes=[pltpu.VMEM((nb, 2, LANES, LANES), F32)],
        compiler_params=_cparams(("parallel", "arbitrary")),
        name="rwkv_scan",
    )(per_batch(als), per_batch(rs), per_batch(bh), per_batch(tg), per_batch(yind), per_batch(bonus),
      per_batch(gate), per_batch(sconst), wc.reshape(bsz, nc, 8, GROUP_W), g)
    return out.reshape(m, GROUP_W)


def _row(v, width=None, offset=0):
    v = v.astype(F32).reshape(-1)
    width = v.shape[0] if width is None else width
    return jnp.zeros((1, width), F32).at[0, offset:offset + v.shape[0]].set(v)


def _w_in_segments():
    gw = GROUP_W
    b0 = 3 * gw
    c0 = b0 + 4 * gw + GROUP_HEADS
    d0 = c0 + 5 * gw + 2 * GROUP_HEADS
    m0 = d0 + 4 * gw + DECAY_RANK + A_RANK
    wide = [0, gw, 2 * gw,
            b0, b0 + gw, b0 + 2 * gw, b0 + 3 * gw + GROUP_HEADS,
            c0, c0 + gw, c0 + 2 * gw, c0 + 3 * gw + 2 * GROUP_HEADS, c0 + 4 * gw + 2 * GROUP_HEADS,
            d0, d0 + gw + DECAY_RANK, d0 + 2 * gw + DECAY_RANK, d0 + 3 * gw + DECAY_RANK + A_RANK,
            m0, m0 + gw]
    narrow = [(b0 + 3 * gw, GROUP_HEADS, SM_BF), (c0 + 3 * gw, 2 * GROUP_HEADS, SM_CI),
              (d0 + gw, DECAY_RANK, SM_WLO), (d0 + 3 * gw + DECAY_RANK, A_RANK, SM_ALO)]
    return wide, narrow, m0 + 2 * gw


def _wlayout_kernel(w_ref, o_ref):
    wide, narrow, _ = _w_in_segments()
    rows = w_ref.shape[0]
    for seg, src in enumerate(wide):
        for half in range(GROUP_W // LANES):
            s = src + half * LANES
            base, off = (s // LANES) * LANES, s % LANES
            win = w_ref[:, base:base + 2 * LANES]
            blk = win[:, :LANES] if off == 0 else pltpu.roll(win, 2 * LANES - off, axis=1)[:, :LANES]
            d = seg * GROUP_W + half * LANES
            o_ref[:, d:d + LANES] = blk.astype(o_ref.dtype)
    lane = _iota((rows, LANES), 1)
    small = jnp.zeros((rows, LANES), F32)
    for src, width, dst in narrow:
        base, off = (src // LANES) * LANES, src % LANES
        win = w_ref[:, base:base + LANES]
        if dst != off:
            win = pltpu.roll(win, (dst - off) % LANES, axis=1)
        small = jnp.where((lane >= dst) & (lane < dst + width), win, small)
    o_ref[:, N_WIDE:] = small.astype(o_ref.dtype)


def _layout_w_in(w_in, rows=256):
    d, n_src = w_in.shape
    assert n_src == _w_in_segments()[2]
    w_pad = jnp.pad(w_in, ((0, 0), (0, NP + LANES - n_src)))
    return pl.pallas_call(
        _wlayout_kernel,
        grid=(d // rows,),
        in_specs=[pl.BlockSpec((rows, NP + LANES), lambda i: (i, 0))],
        out_specs=pl.BlockSpec((rows, NP), lambda i: (i, 0)),
        out_shape=jax.ShapeDtypeStruct((d, NP), BF16),
        compiler_params=_cparams(("parallel",)),
        name="wlayout",
    )(w_pad)


def _layout_mu(mu):
    gw = GROUP_W
    o_w, o_k, o_v, o_a, o_z = gw, gw + DECAY_RANK, 2 * gw + DECAY_RANK, 3 * gw + DECAY_RANK, 3 * gw + DECAY_RANK + A_RANK
    mu_s = jnp.zeros((1, LANES), F32)
    mu_s = mu_s.at[0, SM_WLO:SM_WLO + DECAY_RANK].set(mu[o_w:o_w + DECAY_RANK])
    mu_s = mu_s.at[0, SM_ALO:SM_ALO + A_RANK].set(mu[o_a:o_a + A_RANK])
    return dict(mu_r=_row(mu[0:gw]), mu_k=_row(mu[o_k:o_k + gw]), mu_v=_row(mu[o_v:o_v + gw]),
                mu_z=_row(mu[o_z:o_z + gw]), mu_s=mu_s)


def _pad_rows(w, offset):
    return jnp.zeros((LANES, w.shape[1]), F32).at[offset:offset + w.shape[0]].set(w).astype(BF16)


def _layer(x2, mem2, bsz, n_mem, norm_g, w_in, w_out, sgu_norm_g, sgu_w, sgu_b, fox_q_g, fox_k_g, fox_f_b,
           mlstm_conv_w, mlstm_conv_b, mlstm_i_b, mlstm_f_b, mlstm_out_g,
           rwkv_mu, rwkv_w0, rwkv_w2, rwkv_a0, rwkv_a2, rwkv_k_k, rwkv_k_a, rwkv_r_k, rwkv_ln_g,
           mem_norm_g, mem_w_kv, mem_q_g, mem_k_g):
    tile_h = lambda g: _row(jnp.tile(g, GROUP_HEADS))
    proj = _inproj(x2, _row(norm_g), _layout_w_in(w_in))

    ya = _sgu(proj, _row(sgu_norm_g), sgu_w, jnp.repeat(sgu_b.T, HEAD_DIM, axis=1))

    kaug, qaug_t, v_t = _foxprep(proj, tile_h(fox_q_g), tile_h(fox_k_g), _row(fox_f_b, LANES, SM_BF), bsz)
    yb = _fox(proj, kaug, qaug_t, v_t, bsz)

    cw = jnp.zeros((8, 2 * GROUP_W), F32).at[:CONV_K].set(mlstm_conv_w)
    yc = _mlstm(proj, cw, _row(mlstm_conv_b), _row(mlstm_i_b, LANES, SM_CI), _row(mlstm_f_b, LANES, SM_CF),
                _row(mlstm_out_g), bsz)

    prm = _layout_mu(rwkv_mu)
    prm.update(w0=_row(rwkv_w0), w2p=_pad_rows(rwkv_w2, SM_WLO), a0=_row(rwkv_a0), a2p=_pad_rows(rwkv_a2, SM_ALO),
               k_k=_row(rwkv_k_k), k_a=_row(rwkv_k_a), r_k=_row(rwkv_r_k))
    yd = _rwkv_scan(_rwkv_prep(proj, prm, bsz), _row(rwkv_ln_g), bsz)

    mk, mv = _memkv(mem2, _row(mem_norm_g), mem_w_kv.astype(BF16), tile_h(mem_k_g), n_mem)
    ym = _memattn(proj, mk, mv, tile_h(mem_q_g), bsz, n_mem)

    return _outproj([ya, yb, yc, yd, ym], w_out.astype(BF16), x2)


def kernel(x, mem, norm_g, w_in, w_out, sgu_norm_g, sgu_w, sgu_b, fox_q_g, fox_k_g, fox_f_b, mlstm_conv_w, mlstm_conv_b, mlstm_i_b, mlstm_f_b, mlstm_out_g, rwkv_mu, rwkv_w0, rwkv_w2, rwkv_a0, rwkv_a2, rwkv_k_k, rwkv_k_a, rwkv_r_k, rwkv_ln_g, mem_norm_g, mem_w_kv, mem_q_g, mem_k_g):
    bsz, seq, d = x.shape
    n_mem = mem.shape[1]
    stacked = (norm_g, w_in, w_out, sgu_norm_g, sgu_w, sgu_b, fox_q_g, fox_k_g, fox_f_b,
               mlstm_conv_w, mlstm_conv_b, mlstm_i_b, mlstm_f_b, mlstm_out_g,
               rwkv_mu, rwkv_w0, rwkv_w2, rwkv_a0, rwkv_a2, rwkv_k_k, rwkv_k_a, rwkv_r_k, rwkv_ln_g,
               mem_norm_g, mem_w_kv, mem_q_g, mem_k_g)
    x2 = x.reshape(bsz * seq, d)
    mem2 = mem.reshape(bsz * n_mem, d)
    for layer in range(norm_g.shape[0]):
        x2 = _layer(x2, mem2, bsz, n_mem, *[p[layer] for p in stacked])
    return x2.reshape(bsz, seq, d)
```

```python
import functools

import jax
import jax.numpy as jnp
from jax import lax
from jax.experimental import pallas as pl
from jax.experimental.pallas import tpu as pltpu

F32 = jnp.float32
BF16 = jnp.bfloat16

HEAD_DIM = 64
GROUP_HEADS = 4
GROUP_W = GROUP_HEADS * HEAD_DIM
LANES = 128
CHUNK = 128
CONV_K = 4
EPS = 1e-6
SCALE = HEAD_DIM ** -0.5

CB = dict(a_u=0, a_v=1, a_z=2, b_q=3, b_k=4, b_v=5, b_z=6, c_q=7, c_k=8, c_v=9, c_o=10, c_z=11,
          d_r=12, d_k=13, d_v=14, d_z=15, m_q=16, m_z=17)
N_WIDE = 18 * GROUP_W
SMALL_CB = N_WIDE // LANES
NP = N_WIDE + LANES
SM_BF, SM_CI, SM_CF, SM_WLO, SM_ALO = 0, 4, 8, 16, 32
DECAY_RANK = 16
A_RANK = 16

VMEM_LIMIT = 48 * 1024 * 1024

NN = (((1,), (0,)), ((), ()))
NT = (((1,), (1,)), ((), ()))
TN = (((0,), (0,)), ((), ()))


def _mm(a, b, dims=NN):
    return lax.dot_general(a.astype(BF16), b.astype(BF16), dims, preferred_element_type=F32)


def _split(x, n):
    if x.dtype == BF16 or n == 1:
        return [x.astype(BF16)]
    parts, r = [], x
    for _ in range(n):
        p = r.astype(BF16)
        parts.append(p)
        r = r - p.astype(F32)
    return parts


def _mm_x(a, b, dims=NN, pa=1, pb=1):
    pa_list, pb_list = _split(a, pa), _split(b, pb)
    order = max(len(pa_list), len(pb_list))
    acc = None
    for i, ai in enumerate(pa_list):
        for j, bj in enumerate(pb_list):
            if i + j < order:
                t = lax.dot_general(ai, bj, dims, preferred_element_type=F32)
                acc = t if acc is None else acc + t
    return acc


def _iota(shape, axis):
    return lax.broadcasted_iota(jnp.int32, shape, axis)


def _tril(n, strict=False):
    r, c = _iota((n, n), 0), _iota((n, n), 1)
    return (c < r) if strict else (c <= r)


def _head_sum(x):
    w = x.shape[-1]
    bd = (_iota((w, w), 0) // HEAD_DIM == _iota((w, w), 1) // HEAD_DIM).astype(BF16)
    return _mm_x(x, bd, pa=3)


def _head_rms(x, g):
    ms = _head_sum(x * x) * (1.0 / HEAD_DIM)
    return x * lax.rsqrt(ms + EPS) * g


def _lo_mask(shape):
    return _iota(shape, len(shape) - 1) % LANES < HEAD_DIM


def _cparams(sem):
    return pltpu.CompilerParams(dimension_semantics=sem, vmem_limit_bytes=VMEM_LIMIT)


def _inproj_kernel(x_ref, g_ref, w_ref, o_ref, *, col_chunk):
    x = x_ref[...]
    ms = jnp.mean(x * x, axis=-1, keepdims=True)
    h = (x * lax.rsqrt(ms + EPS) * g_ref[...]).astype(BF16)
    n = o_ref.shape[1]
    for c0 in range(0, n, col_chunk):
        c1 = min(c0 + col_chunk, n)
        o_ref[:, c0:c1] = jnp.dot(h, w_ref[:, c0:c1], preferred_element_type=F32)


def _inproj(x2, g, w_bf16, tm=256):
    m, d = x2.shape
    n = w_bf16.shape[1]
    return pl.pallas_call(
        functools.partial(_inproj_kernel, col_chunk=512),
        grid=(m // tm,),
        in_specs=[pl.BlockSpec((tm, d), lambda i: (i, 0)),
                  pl.BlockSpec((1, d), lambda i: (0, 0)),
                  pl.BlockSpec((d, n), lambda i: (0, 0))],
        out_specs=pl.BlockSpec((tm, n), lambda i: (i, 0)),
        out_shape=jax.ShapeDtypeStruct((m, n), F32),
        compiler_params=_cparams(("parallel",)),
        name="inproj",
    )(x2, g, w_bf16)


def _outproj_kernel(ya, yb, yc, yd, ym, w_ref, x_ref, o_ref):
    mixed = jnp.concatenate([ya[...], yb[...], yc[...], yd[...], ym[...]], axis=1)
    o_ref[...] = x_ref[...] + jnp.dot(mixed, w_ref[...], preferred_element_type=F32)


def _outproj(ys, w_bf16, x2, tm=1024):
    m, d = x2.shape
    yspec = pl.BlockSpec((tm, GROUP_W), lambda i: (i, 0))
    return pl.pallas_call(
        _outproj_kernel,
        grid=(m // tm,),
        in_specs=[yspec] * 5 + [pl.BlockSpec(w_bf16.shape, lambda i: (0, 0)),
                                pl.BlockSpec((tm, d), lambda i: (i, 0))],
        out_specs=pl.BlockSpec((tm, d), lambda i: (i, 0)),
        out_shape=jax.ShapeDtypeStruct((m, d), F32),
        compiler_params=_cparams(("parallel",)),
        name="outproj",
    )(*ys, w_bf16, x2)


def _sgu_kernel(u_ref, v_ref, z_ref, g_ref, w_ref, bias_ref, o_ref):
    t = u_ref.shape[0]
    u = jax.nn.gelu(u_ref[...])
    vn = _head_rms(jax.nn.gelu(v_ref[...]), g_ref[...])
    gate = jax.nn.silu(z_ref[...])
    tril = _tril(CHUNK)
    lo = _lo_mask((CHUNK, LANES))
    ws = [jnp.where(tril, w_ref[g], 0.0).astype(BF16) for g in range(GROUP_HEADS)]
    bias = bias_ref[...]
    for c in range(t // CHUNK):
        sl = slice(c * CHUNK, (c + 1) * CHUNK)
        tiles = []
        for p in range(2):
            vt = vn[sl, p * LANES:(p + 1) * LANES]
            tiles.append(_mm(ws[2 * p], jnp.where(lo, vt, 0.0)) + _mm(ws[2 * p + 1], jnp.where(lo, 0.0, vt)))
        mixed = jnp.concatenate(tiles, axis=1) + bias
        o_ref[sl, :] = (u[sl] * mixed * gate[sl]).astype(o_ref.dtype)


def _sgu(proj, g, w, bias_tile, t=512):
    m = proj.shape[0]
    col = lambda cb: pl.BlockSpec((t, GROUP_W), lambda i, cb=cb: (i, cb))
    full = lambda a: pl.BlockSpec(a.shape, lambda i, n=a.ndim: (0,) * n)
    return pl.pallas_call(
        _sgu_kernel,
        grid=(m // t,),
        in_specs=[col(CB["a_u"]), col(CB["a_v"]), col(CB["a_z"]), full(g), full(w), full(bias_tile)],
        out_specs=pl.BlockSpec((t, GROUP_W), lambda i: (i, 0)),
        out_shape=jax.ShapeDtypeStruct((m, GROUP_W), BF16),
        compiler_params=_cparams(("parallel",)),
        name="sgu",
    )(proj, proj, proj, g, w, bias_tile)


def _memkv_kernel(mem_ref, g_ref, w_ref, kg_ref, k_ref, v_ref):
    x = mem_ref[...]
    ms = jnp.mean(x * x, axis=-1, keepdims=True)
    h = (x * lax.rsqrt(ms + EPS) * g_ref[...]).astype(BF16)
    kv = jnp.dot(h, w_ref[...], preferred_element_type=F32)
    k_ref[...] = _head_rms(kv[:, :GROUP_W], kg_ref[...]).astype(BF16)
    v_ref[...] = kv[:, GROUP_W:].astype(BF16)


def _memkv(mem2, g, w_bf16, kg, n_mem):
    m, d = mem2.shape
    full = lambda a: pl.BlockSpec(a.shape, lambda i, n=a.ndim: (0,) * n)
    out = pl.BlockSpec((n_mem, GROUP_W), lambda i: (i, 0))
    return pl.pallas_call(
        _memkv_kernel,
        grid=(m // n_mem,),
        in_specs=[pl.BlockSpec((n_mem, d), lambda i: (i, 0)), full(g), full(w_bf16), full(kg)],
        out_specs=[out, out],
        out_shape=[jax.ShapeDtypeStruct((m, GROUP_W), BF16)] * 2,
        compiler_params=_cparams(("parallel",)),
        name="memkv",
    )(mem2, g, w_bf16, kg)


def _memattn_kernel(q_ref, z_ref, k_ref, v_ref, qg_ref, o_ref):
    q = (_head_rms(q_ref[...], qg_ref[...]) * SCALE).astype(BF16)
    k = k_ref[...]
    v = v_ref[...]
    t = q.shape[0]
    lo_q = _lo_mask((t, LANES))
    lo_v = _lo_mask((v.shape[0], LANES))
    zero = jnp.zeros((), BF16)
    heads = range(GROUP_HEADS)
    sls = [slice((h // 2) * LANES, (h // 2 + 1) * LANES) for h in heads]
    s = [_mm(jnp.where(lo_q if h % 2 == 0 else ~lo_q, q[:, sls[h]], zero), k[:, sls[h]], NT) for h in heads]
    e = [jnp.exp(s[h] - jnp.max(s[h], axis=-1, keepdims=True)) for h in heads]
    pr = [e[h] / jnp.sum(e[h], axis=-1, keepdims=True) for h in heads]
    pv = [_mm(pr[h], jnp.where(lo_v if h % 2 == 0 else ~lo_v, v[:, sls[h]], zero)) for h in heads]
    y = jnp.concatenate([pv[0] + pv[1], pv[2] + pv[3]], axis=1)
    o_ref[...] = (y * jax.nn.silu(z_ref[...])).astype(o_ref.dtype)


def _memattn(proj, kn, vb, qg, bsz, n_mem, t=512):
    m = proj.shape[0]
    nt = m // bsz // t
    col = lambda cb: pl.BlockSpec((t, GROUP_W), lambda b, i, cb=cb: (b * nt + i, cb))
    kv = pl.BlockSpec((n_mem, GROUP_W), lambda b, i: (b, 0))
    return pl.pallas_call(
        _memattn_kernel,
        grid=(bsz, nt),
        in_specs=[col(CB["m_q"]), col(CB["m_z"]), kv, kv, pl.BlockSpec(qg.shape, lambda b, i: (0, 0))],
        out_specs=pl.BlockSpec((t, GROUP_W), lambda b, i: (b * nt + i, 0)),
        out_shape=jax.ShapeDtypeStruct((m, GROUP_W), BF16),
        compiler_params=_cparams(("parallel", "parallel")),
        name="memattn",
    )(proj, proj, kn, vb, qg)


LOG2E = 1.4426950408889634
F_TERMS = 3
V_ROWS = 80


def _bf16_terms(x, n):
    terms, r = [], x
    for _ in range(n):
        p = r.astype(BF16).astype(F32)
        terms.append(p)
        r = r - p
    return terms


def _foxprep_kernel(q_ref, k_ref, v_ref, sm_ref, qg_ref, kg_ref, fb_ref,
                    kaug_ref, qaug_t_ref, v_t_ref, carry_ref):
    @pl.when(pl.program_id(1) == 0)
    def _():
        carry_ref[...] = jnp.zeros_like(carry_ref)

    t = q_ref.shape[0]
    qn = _head_rms(q_ref[...], qg_ref[...]) * (SCALE * LOG2E)
    kn = _head_rms(k_ref[...], kg_ref[...])
    lf = jax.nn.log_sigmoid(sm_ref[...] + fb_ref[...])
    cs = _mm_x(_tril(t).astype(BF16), lf, pb=3) + carry_ref[0:1, :]
    carry_ref[...] = jnp.broadcast_to(cs[t - 1:t, :], carry_ref.shape)
    f2 = cs * LOG2E

    lane = _iota((t, LANES), 1)
    q_tiles = []
    for h in range(GROUP_HEADS):
        sl = slice((h // 2) * LANES, (h // 2 + 1) * LANES)
        q_t, k_t = qn[:, sl], kn[:, sl]
        if h % 2 == 1:
            q_t = pltpu.roll(q_t, HEAD_DIM, axis=1)
            k_t = pltpu.roll(k_t, HEAD_DIM, axis=1)
        terms = _bf16_terms(f2[:, SM_BF + h:SM_BF + h + 1], F_TERMS)
        qa = jnp.where(lane < HEAD_DIM + 2 * F_TERMS, 1.0, 0.0)
        ka = qa
        for n, term in enumerate(terms):
            qa = jnp.where(lane == HEAD_DIM + n, term, qa)
            ka = jnp.where(lane == HEAD_DIM + F_TERMS + n, -term, ka)
        qa = jnp.where(lane < HEAD_DIM, q_t, qa)
        ka = jnp.where(lane < HEAD_DIM, k_t, ka)
        kaug_ref[:, h * LANES:(h + 1) * LANES] = ka.astype(BF16)
        q_tiles.append(qa)
    qaug_t_ref[0] = jnp.concatenate(q_tiles, axis=1).T.astype(BF16)
    v_t = v_ref[...].T
    pad_rows = jnp.where(_iota((V_ROWS - HEAD_DIM, t), 0) == 0, 1.0, 0.0)
    v_t_ref[0] = jnp.concatenate(
        [x for h in range(GROUP_HEADS) for x in (v_t[h * HEAD_DIM:(h + 1) * HEAD_DIM], pad_rows)],
        axis=0).astype(BF16)


def _foxprep(proj, qg, kg, fb_row, bsz, t=512):
    m = proj.shape[0]
    seq = m // bsz
    nt = seq // t
    col = lambda cb: pl.BlockSpec((t, GROUP_W), lambda b, i, cb=cb: (b * nt + i, cb))
    row = lambda a: pl.BlockSpec(a.shape, lambda b, i: (0, 0))
    aug_w = GROUP_HEADS * LANES
    return pl.pallas_call(
        _foxprep_kernel,
        grid=(bsz, nt),
        in_specs=[col(CB["b_q"]), col(CB["b_k"]), col(CB["b_v"]),
                  pl.BlockSpec((t, LANES), lambda b, i: (b * nt + i, SMALL_CB)),
                  row(qg), row(kg), row(fb_row)],
        out_specs=[pl.BlockSpec((t, aug_w), lambda b, i: (b * nt + i, 0)),
                   pl.BlockSpec((1, aug_w, t), lambda b, i: (b, 0, i)),
                   pl.BlockSpec((1, GROUP_HEADS * V_ROWS, t), lambda b, i: (b, 0, i))],
        out_shape=[jax.ShapeDtypeStruct((m, aug_w), BF16),
                   jax.ShapeDtypeStruct((bsz, aug_w, seq), BF16),
                   jax.ShapeDtypeStruct((bsz, GROUP_HEADS * V_ROWS, seq), BF16)],
        scratch_shapes=[pltpu.VMEM((8, LANES), F32)],
        compiler_params=_cparams(("parallel", "arbitrary")),
        name="foxprep",
    )(proj, proj, proj, proj, qg, kg, fb_row)


def _fox_kernel(q_ref, k_ref, v_ref, z_ref, o_ref, m_scr, acc_scr, sa_scr, sb_scr, mxa_scr, mxb_scr,
                *, tq, tk):
    i = pl.program_id(2)
    m_scr[...] = jnp.full_like(m_scr, -jnp.inf)
    acc_scr[...] = jnp.zeros_like(acc_scr)

    def scores(j, s_scr, mx_scr):
        k0 = pl.multiple_of(j * tk, tk)
        for hh in range(2):
            k_blk = k_ref[pl.ds(k0, tk), hh * LANES:(hh + 1) * LANES]
            s_t = jnp.dot(k_blk, q_ref[0, hh * LANES:(hh + 1) * LANES, :],
                          preferred_element_type=F32)
            s_scr[hh] = s_t
            mx_scr[hh] = jnp.broadcast_to(jnp.max(s_t, axis=0, keepdims=True), (8, tq))

    def softmax_pv(j, s_scr, mx_scr, masked=False):
        k0 = pl.multiple_of(j * tk, tk)
        for hh in range(2):
            s_t = s_scr[hh]
            mx = mx_scr[hh][0:1, :]
            if masked:
                visible = (k0 + _iota((tk, tq), 0)) <= (i * tq + _iota((tk, tq), 1))
                s_t = jnp.where(visible, s_t, -jnp.inf)
                mx = jnp.max(s_t, axis=0, keepdims=True)
            m_old = m_scr[hh][0:1, :]
            m_new = jnp.maximum(m_old, mx)
            alpha = jnp.exp2(m_old - m_new)
            p_t = jnp.exp2(s_t - m_new)
            m_scr[hh] = jnp.broadcast_to(m_new, (8, tq))
            v_blk = v_ref[0, hh * V_ROWS:(hh + 1) * V_ROWS, pl.ds(k0, tk)]
            acc_scr[hh] = acc_scr[hh] * alpha + jnp.dot(v_blk, p_t.astype(BF16), preferred_element_type=F32)

    assert tq == 2 * tk
    n_full = 2 * i
    scores(0, sa_scr, mxa_scr)

    def two_blocks(jj, carry):
        scores(2 * jj + 1, sb_scr, mxb_scr)
        softmax_pv(2 * jj, sa_scr, mxa_scr)
        scores(2 * jj + 2, sa_scr, mxa_scr)
        softmax_pv(2 * jj + 1, sb_scr, mxb_scr)
        return carry

    lax.fori_loop(0, i, two_blocks, 0)
    scores(n_full + 1, sb_scr, mxb_scr)
    softmax_pv(n_full, sa_scr, mxa_scr, masked=True)
    softmax_pv(n_full + 1, sb_scr, mxb_scr, masked=True)

    y_t = jnp.concatenate([acc_scr[hh][:HEAD_DIM] / acc_scr[hh][HEAD_DIM:HEAD_DIM + 1] for hh in range(2)],
                          axis=0)
    o_ref[...] = (y_t.T * jax.nn.silu(z_ref[...])).astype(o_ref.dtype)


def _fox(proj, kaug, qaug_t, v_t, bsz, tq=512, tk=256):
    m = proj.shape[0]
    seq = m // bsz
    nq = seq // tq
    s_buf = pltpu.VMEM((2, tk, tq), F32)
    mx_buf = pltpu.VMEM((2, 8, tq), F32)
    return pl.pallas_call(
        functools.partial(_fox_kernel, tq=tq, tk=tk),
        grid=(bsz, 2, nq),
        in_specs=[pl.BlockSpec((1, 2 * LANES, tq), lambda b, p, i: (b, p, i)),
                  pl.BlockSpec((seq, 2 * LANES), lambda b, p, i: (b, p)),
                  pl.BlockSpec((1, 2 * V_ROWS, seq), lambda b, p, i: (b, p, 0)),
                  pl.BlockSpec((tq, LANES), lambda b, p, i: (b * nq + i, 2 * CB["b_z"] + p))],
        out_specs=pl.BlockSpec((tq, LANES), lambda b, p, i: (b * nq + i, p)),
        out_shape=jax.ShapeDtypeStruct((m, GROUP_W), BF16),
        scratch_shapes=[pltpu.VMEM((2, 8, tq), F32), pltpu.VMEM((2, V_ROWS, tq), F32),
                        s_buf, s_buf, mx_buf, mx_buf],
        compiler_params=_cparams(("parallel", "parallel", "arbitrary")),
        name="fox",
    )(qaug_t, kaug, v_t, proj)


def _mlstm_kernel(q_ref, k_ref, v_ref, og_ref, z_ref, sm_ref, cw_ref, cb_ref, ib_ref, fb_ref, g_ref,
                  out_ref, xbuf, c_scr, n_scr, m_scr):
    t = CHUNK
    nb = q_ref.shape[0]

    @pl.when(pl.program_id(1) == 0)
    def _():
        xbuf[:, 0:8, :] = jnp.zeros((nb, 8, 2 * GROUP_W), F32)
        c_scr[...] = jnp.zeros_like(c_scr)
        n_scr[...] = jnp.zeros_like(n_scr)
        m_scr[...] = jnp.zeros_like(m_scr)

    tril = _tril(t)
    tril_bf = tril.astype(BF16)
    lo = _lo_mask((t, LANES))
    lo_row = _lo_mask((1, LANES))
    blockdiag = (_iota((LANES, LANES), 0) < HEAD_DIM) == (_iota((LANES, LANES), 1) < HEAD_DIM)
    lane_row = _iota((1, LANES), 1)
    sls = [slice(p * LANES, (p + 1) * LANES) for p in range(2)]

    qs, ks, vs, lis, bcums, b_ts, i_ts, m_rows = [], [], [], [], [], [], [], []
    for b in range(nb):
        xbuf[b, 8:8 + t, 0:GROUP_W] = q_ref[b]
        xbuf[b, 8:8 + t, GROUP_W:] = k_ref[b]
        conv = cb_ref[...]
        for jj in range(CONV_K):
            conv = conv + cw_ref[jj:jj + 1, :] * xbuf[b, 8 - (CONV_K - 1) + jj:8 - (CONV_K - 1) + jj + t, :]
        xbuf[b, 0:8, :] = xbuf[b, t:t + 8, :]
        qk_act = jax.nn.silu(conv)
        qs.append(qk_act[:, :GROUP_W])
        ks.append(qk_act[:, GROUP_W:] * SCALE)
        vs.append(v_ref[b])
        sm = sm_ref[b]
        li = sm + ib_ref[...]
        lf = jax.nn.log_sigmoid(sm + fb_ref[...])
        bcum = _mm_x(tril_bf, lf, pb=3)
        lis.append(li)
        bcums.append(bcum)
        b_ts.append(bcum.T)
        i_ts.append(li.T)
        m_rows.append(m_scr[b, 0:1, :])

    units = [(b, h) for b in range(nb) for h in range(GROUP_HEADS)]
    pairs = [(b, p) for b in range(nb) for p in range(2)]
    idx = range(len(units))
    pair_of = [2 * b + h // 2 for b, h in units]
    masks = [lo if h % 2 == 0 else ~lo for _, h in units]
    qp = [qs[b][:, sls[p]] for b, p in pairs]
    kp = [ks[b][:, sls[p]] for b, p in pairs]
    vp = [vs[b][:, sls[p]] for b, p in pairs]
    c_ps = [c_scr[b, p] for b, p in pairs]
    n_ps = [n_scr[b, p][0:1, :] for b, p in pairs]
    qk = [_mm(jnp.where(masks[u], qp[pair_of[u]], 0.0), kp[pair_of[u]], NT) for u in idx]
    q_c = [_mm(qp[i], c_ps[i]) for i in range(len(pairs))]
    q_n = [qp[i] * n_ps[i] for i in range(len(pairs))]
    b_c = [bcums[b][:, SM_CF + h:SM_CF + h + 1] for b, h in units]
    b_r = [b_ts[b][SM_CF + h:SM_CF + h + 1, :] for b, h in units]
    i_c = [lis[b][:, SM_CI + h:SM_CI + h + 1] for b, h in units]
    i_r = [i_ts[b][SM_CI + h:SM_CI + h + 1, :] for b, h in units]
    m_st = [m_rows[b][:, h:h + 1] for b, h in units]
    log_d = [jnp.where(tril, b_c[u] - b_r[u] + i_r[u], -jnp.inf) for u in idx]
    inter = [b_c[u] + m_st[u] for u in idx]
    m_t = [jnp.maximum(jnp.max(log_d[u], axis=-1, keepdims=True), inter[u]) for u in idx]
    s = [qk[u] * jnp.exp(log_d[u] - m_t[u]) for u in idx]
    w_inter = [jnp.exp(inter[u] - m_t[u]) for u in idx]
    sv = [_mm(s[u], jnp.where(masks[u], vp[pair_of[u]], 0.0)) for u in idx]
    num = [sv[u] + w_inter[u] * jnp.where(masks[u], q_c[pair_of[u]], 0.0) for u in idx]
    den = [jnp.sum(s[u], axis=-1, keepdims=True)
           + w_inter[u] * jnp.sum(jnp.where(masks[u], q_n[pair_of[u]], 0.0), axis=-1, keepdims=True) for u in idx]
    hv = [num[u] / jnp.maximum(jnp.abs(den[u]), jnp.exp(-m_t[u])) for u in idx]
    g = [b_c[u][t - 1:t, :] for u in idx]
    m_new = [jnp.maximum(g[u] + m_st[u], jnp.max(g[u] - b_r[u] + i_r[u], axis=-1, keepdims=True)) for u in idx]
    w_col = [jnp.exp(g[u] - b_c[u] + i_c[u] - m_new[u]) for u in idx]
    cd = [jnp.exp(g[u] + m_st[u] - m_new[u]) for u in idx]
    first = lambda b, p: GROUP_HEADS * b + 2 * p
    kw = [kp[i] * jnp.where(lo, w_col[first(b, p)], w_col[first(b, p) + 1]) for i, (b, p) in enumerate(pairs)]
    cd_row = [jnp.where(lo_row, cd[first(b, p)], cd[first(b, p) + 1]) for b, p in pairs]
    upd = [_mm(kw[i], vp[i], TN) for i in range(len(pairs))]
    for i, (b, p) in enumerate(pairs):
        c_scr[b, p] = cd_row[i] * c_ps[i] + jnp.where(blockdiag, upd[i], 0.0)
        n_new = cd_row[i] * n_ps[i] + jnp.sum(kw[i], axis=0, keepdims=True)
        n_scr[b, p] = jnp.broadcast_to(n_new, (8, LANES))
    for b in range(nb):
        m_row_new = m_rows[b]
        for h in range(GROUP_HEADS):
            m_row_new = jnp.where(lane_row == h, m_new[GROUP_HEADS * b + h], m_row_new)
        m_scr[b] = jnp.broadcast_to(m_row_new, (8, LANES))
        h_tiles = [hv[first(b, p)] + hv[first(b, p) + 1] for p in range(2)]
        hcat = jax.nn.sigmoid(og_ref[b]) * jnp.concatenate(h_tiles, axis=1)
        y = _head_rms(hcat, g_ref[...])
        out_ref[b] = (y * jax.nn.silu(z_ref[b])).astype(out_ref.dtype)


def _mlstm(proj, cw, cb, ib_row, fb_row, g, bsz, nb=4):
    m = proj.shape[0]
    seq = m // bsz
    nc = seq // CHUNK
    nb = min(nb, bsz)
    proj3 = proj.reshape(bsz, seq, proj.shape[1])
    col = lambda cb_: pl.BlockSpec((nb, CHUNK, GROUP_W), lambda b, c, cb_=cb_: (b, c, cb_))
    full = lambda a: pl.BlockSpec(a.shape, lambda b, c, n=a.ndim: (0,) * n)
    out = pl.pallas_call(
        _mlstm_kernel,
        grid=(bsz // nb, nc),
        in_specs=[col(CB["c_q"]), col(CB["c_k"]), col(CB["c_v"]), col(CB["c_o"]), col(CB["c_z"]),
                  pl.BlockSpec((nb, CHUNK, LANES), lambda b, c: (b, c, SMALL_CB)),
                  full(cw), full(cb), full(ib_row), full(fb_row), full(g)],
        out_specs=pl.BlockSpec((nb, CHUNK, GROUP_W), lambda b, c: (b, c, 0)),
        out_shape=jax.ShapeDtypeStruct((bsz, seq, GROUP_W), BF16),
        scratch_shapes=[pltpu.VMEM((nb, CHUNK + 8, 2 * GROUP_W), F32), pltpu.VMEM((nb, 2, LANES, LANES), F32),
                        pltpu.VMEM((nb, 2, 8, LANES), F32), pltpu.VMEM((nb, 8, LANES), F32)],
        compiler_params=_cparams(("parallel", "arbitrary")),
        name="mlstm",
    )(proj3, proj3, proj3, proj3, proj3, proj3, cw, cb, ib_row, fb_row, g)
    return out.reshape(m, GROUP_W)


PASSES_AA = 1
PASSES_INV = 1
PASSES_STATE = 1


def _mmp(a, b, dims=NN, passes=1):
    return _mm_x(a, b, dims, pa=passes, pb=passes)


def _tri_inverse(mats):
    n = mats[0].shape[0]
    r, c = _iota((n, n), 0), _iota((n, n), 1)
    same = lambda b: (r // b) == (c // b)
    mm = lambda x, y: _mmp(x, y, passes=PASSES_INV)
    eye = jnp.where(r == c, 1.0, 0.0)
    a8 = [jnp.where(same(8), a, 0.0) for a in mats]
    ts = [eye + x for x in a8]
    ps = [mm(x, x) for x in a8]
    ts = [t + mm(t, p) for t, p in zip(ts, ps)]
    ps = [mm(p, p) for p in ps]
    ts = [t + mm(t, p) for t, p in zip(ts, ps)]
    b = 8
    while b < n:
        off = same(2 * b) & ~same(b)
        inner = [mm(jnp.where(off, a, 0.0), t) for a, t in zip(mats, ts)]
        ts = [t + mm(t, x) for t, x in zip(ts, inner)]
        b *= 2
    return ts


def _rwkv_prep_kernel(r_ref, k_ref, v_ref, z_ref, sm_ref, pr_ref, pk_ref, pv_ref, pz_ref, psm_ref,
                      mur_ref, muk_ref, muv_ref, muz_ref, mus_ref,
                      w0_ref, w2_ref, a0_ref, a2_ref, kk_ref, ka_ref, rk_ref,
                      als_ref, rs_ref, bh_ref, tg_ref, uind_ref, yind_ref, bonus_ref, gate_ref,
                      sconst_ref, wc_ref, *, steps_per_seq):
    t = CHUNK
    first = (pl.program_id(0) % steps_per_seq) == 0
    row0 = _iota((r_ref.shape[0], 1), 0) == 0

    def shifted(x_ref, p_ref, mu_ref):
        x = x_ref[...]
        prev_row = jnp.where(first, 0.0, p_ref[7:8, :])
        prev = jnp.where(row0, prev_row, pltpu.roll(x, 1, axis=0))
        return x + mu_ref[...] * (prev - x)

    r = shifted(r_ref, pr_ref, mur_ref)
    k = shifted(k_ref, pk_ref, muk_ref)
    v = shifted(v_ref, pv_ref, muv_ref)
    z = shifted(z_ref, pz_ref, muz_ref)
    sm = shifted(sm_ref, psm_ref, mus_ref)

    w_log = -jax.nn.softplus(-(w0_ref[...] + _mm(jnp.tanh(sm), w2_ref[...]))) - 0.5
    ld = -jnp.exp(w_log)
    a = jax.nn.sigmoid(a0_ref[...] + _mm(sm, a2_ref[...]))
    kk = k * kk_ref[...]
    kk = kk / jnp.maximum(jnp.sqrt(_head_sum(kk * kk)), 1e-12)
    k2 = k * (1.0 + (a - 1.0) * ka_ref[...])
    kka = kk * a

    rows = r.shape[0]
    n_sub = rows // t
    rr, cc = _iota((rows, rows), 0), _iota((rows, rows), 1)
    chunk_tril = ((cc <= rr) & (rr // t == cc // t)).astype(BF16)
    lw = _mm_x(chunk_tril, ld, pb=3)
    lw_ex = lw - ld
    per_chunk = lambda row_of: jnp.concatenate(
        [jnp.broadcast_to(lw[c * t + row_of:c * t + row_of + 1, :], (t, GROUP_W)) for c in range(n_sub)], axis=0)
    lw_mid = per_chunk(t // 2 - 1)
    lw_end = per_chunk(t - 1)
    e_in = jnp.exp(lw - lw_mid)
    e_out = jnp.exp(lw_mid - lw)
    al_m = -kk * jnp.exp(lw_ex - lw_mid)
    r_m = r * e_in
    be_m = kka * e_out
    k_m = k2 * e_out
    al_s = -kk * jnp.exp(lw_ex)
    r_s = r * jnp.exp(lw)
    e_end = jnp.exp(lw_end - lw)
    b_h = kka * e_end
    k_h = k2 * e_end

    als_ref[...] = al_s.astype(BF16)
    rs_ref[...] = r_s.astype(BF16)
    bh_ref[...] = b_h.astype(BF16)
    bonus_ref[...] = _head_sum(r * k2 * rk_ref[...]) * v
    gate_ref[...] = jax.nn.silu(z)
    for c in range(n_sub):
        wc_ref[c] = jnp.exp(lw_end[c * t:c * t + 8, :])

    strict = _tril(t, strict=True)
    incl = _tril(t)
    lo = _lo_mask((t, LANES))
    units = [(c, h) for c in range(n_sub) for h in range(GROUP_HEADS)]
    idx = range(len(units))
    rsl = [slice(c * t, (c + 1) * t) for c, _ in units]
    csl = [slice((h // 2) * LANES, (h // 2 + 1) * LANES) for _, h in units]
    masks = [lo if h % 2 == 0 else ~lo for _, h in units]
    mm = lambda x, y, dims=NN: _mmp(x, y, dims, passes=PASSES_INV)
    sel = lambda u, x: jnp.where(masks[u], x[rsl[u], csl[u]], 0.0)
    aa = [_mmp(jnp.concatenate([sel(u, al_m), sel(u, r_m)], axis=0),
               jnp.concatenate([be_m[rsl[u], csl[u]], k_m[rsl[u], csl[u]]], axis=0), NT, passes=PASSES_AA)
          for u in idx]
    a_ab = [jnp.where(strict, x[:t, :t], 0.0) for x in aa]
    a_ak = [jnp.where(strict, x[:t, t:], 0.0) for x in aa]
    a_rb = [jnp.where(incl, x[t:, :t], 0.0) for x in aa]
    a_rk = [jnp.where(incl, x[t:, t:], 0.0) for x in aa]
    v_h = [sel(u, v) for u in idx]
    av = [mm(a_ak[u], v_h[u]) for u in idx]
    y_rk = [mm(a_rk[u], v_h[u]) for u in idx]
    s_vk = [mm(v_h[u], sel(u, k_h), TN) for u in idx]
    tinv = _tri_inverse(a_ab)
    u_ind = [mm(tinv[u], av[u]) for u in idx]
    g_mat = [mm(a_rb[u], tinv[u]) for u in idx]
    y_ind = [mm(a_rb[u], u_ind[u]) + y_rk[u] for u in idx]
    s_c = [mm(u_ind[u], sel(u, b_h), TN) + s_vk[u] for u in idx]
    for u, (c, h) in enumerate(units):
        tg_ref[rsl[u], h * 2 * t:h * 2 * t + t] = tinv[u].astype(BF16)
        tg_ref[rsl[u], h * 2 * t + t:(h + 1) * 2 * t] = g_mat[u].astype(BF16)
    for c in range(n_sub):
        for p in range(2):
            u0 = c * GROUP_HEADS + 2 * p
            rs_, sl = slice(c * t, (c + 1) * t), slice(p * LANES, (p + 1) * LANES)
            uind_ref[rs_, sl] = u_ind[u0] + u_ind[u0 + 1]
            yind_ref[rs_, sl] = y_ind[u0] + y_ind[u0 + 1]
            sconst_ref[rs_, sl] = s_c[u0] + s_c[u0 + 1]


def _rwkv_prep(proj, prm, bsz, n_sub=4):
    m = proj.shape[0]
    rows = n_sub * CHUNK
    sub = rows // 8
    col = lambda cb: pl.BlockSpec((rows, GROUP_W), lambda i, cb=cb: (i, cb))
    prev = lambda cb: pl.BlockSpec((8, GROUP_W), lambda i, cb=cb: (jnp.maximum(i * sub - 1, 0), cb))
    full = lambda a: pl.BlockSpec(a.shape, lambda i, n=a.ndim: (0,) * n)
    wide = pl.BlockSpec((rows, GROUP_W), lambda i: (i, 0))
    params = [prm["mu_r"], prm["mu_k"], prm["mu_v"], prm["mu_z"], prm["mu_s"], prm["w0"], prm["w2p"],
              prm["a0"], prm["a2p"], prm["k_k"], prm["k_a"], prm["r_k"]]
    wide_bf = jax.ShapeDtypeStruct((m, GROUP_W), BF16)
    wide_f = jax.ShapeDtypeStruct((m, GROUP_W), F32)
    return pl.pallas_call(
        functools.partial(_rwkv_prep_kernel, steps_per_seq=m // bsz // rows),
        grid=(m // rows,),
        in_specs=[col(CB["d_r"]), col(CB["d_k"]), col(CB["d_v"]), col(CB["d_z"]),
                  pl.BlockSpec((rows, LANES), lambda i: (i, SMALL_CB)),
                  prev(CB["d_r"]), prev(CB["d_k"]), prev(CB["d_v"]), prev(CB["d_z"]),
                  pl.BlockSpec((8, LANES), lambda i: (jnp.maximum(i * sub - 1, 0), SMALL_CB))]
        + [full(a) for a in params],
        out_specs=[wide, wide, wide,
                   pl.BlockSpec((rows, 4 * GROUP_W), lambda i: (i, 0)),
                   wide, wide, wide, wide, wide,
                   pl.BlockSpec((n_sub, 8, GROUP_W), lambda i: (i, 0, 0))],
        out_shape=[wide_bf, wide_bf, wide_bf, jax.ShapeDtypeStruct((m, 4 * GROUP_W), BF16),
                   wide_f, wide_f, wide_f, wide_f, wide_f,
                   jax.ShapeDtypeStruct((m // CHUNK, 8, GROUP_W), F32)],
        compiler_params=_cparams(("parallel",)),
        name="rwkv_prep",
    )(proj, proj, proj, proj, proj, proj, proj, proj, proj, proj, *params)


def _rwkv_scan_kernel(als_ref, rs_ref, bh_ref, tg_ref, yind_ref, bonus_ref, gate_ref, sconst_ref, wc_ref,
                      g_ref, out_ref, s_scr):
    t = CHUNK

    @pl.when(pl.program_id(1) == 0)
    def _():
        s_scr[...] = jnp.zeros_like(s_scr)

    nb = als_ref.shape[0]
    lo = _lo_mask((t, LANES))
    blockdiag = (_iota((LANES, LANES), 0) < HEAD_DIM) == (_iota((LANES, LANES), 1) < HEAD_DIM)
    zero = jnp.zeros((), F32)
    sls = [slice(p * LANES, (p + 1) * LANES) for p in range(2)]
    pairs = [(b, p) for b in range(nb) for p in range(2)]
    units = [(b, h) for b in range(nb) for h in range(GROUP_HEADS)]
    s_ps = [s_scr[b, p] for b, p in pairs]
    x = [_mm_x(jnp.concatenate([als_ref[b, :, sls[p]], rs_ref[b, :, sls[p]]], axis=0), s_ps[i], NT,
               pb=PASSES_STATE) for i, (b, p) in enumerate(pairs)]
    tg = [jnp.concatenate([tg_ref[b, :, h * 2 * t:h * 2 * t + t], tg_ref[b, :, h * 2 * t + t:(h + 1) * 2 * t]],
                          axis=0) for b, h in units]
    res = [_mm_x(tg[u], jnp.where(lo if h % 2 == 0 else ~lo, x[2 * b + h // 2][:t], zero), pb=PASSES_STATE)
           for u, (b, h) in enumerate(units)]
    tx = [res[GROUP_HEADS * b + 2 * p][:t] + res[GROUP_HEADS * b + 2 * p + 1][:t] for b, p in pairs]
    gx = [res[GROUP_HEADS * b + 2 * p][t:] + res[GROUP_HEADS * b + 2 * p + 1][t:] for b, p in pairs]
    upd = [_mm_x(tx[i], bh_ref[b, :, sls[p]], TN, pa=PASSES_STATE) for i, (b, p) in enumerate(pairs)]
    for i, (b, p) in enumerate(pairs):
        s_scr[b, p] = (s_ps[i] * wc_ref[b, 0][0:1, sls[p]] + sconst_ref[b, :, sls[p]]
                       + jnp.where(blockdiag, upd[i], zero))
    for b in range(nb):
        y_tiles = [x[2 * b + p][t:] + gx[2 * b + p] + yind_ref[b, :, sls[p]] for p in range(2)]
        y = _head_rms(jnp.concatenate(y_tiles, axis=1), g_ref[...]) + bonus_ref[b]
        out_ref[b] = (y * gate_ref[b]).astype(out_ref.dtype)


def _rwkv_scan(prep, g, bsz, nb=4):
    als, rs, bh, tg, _uind, yind, bonus, gate, sconst, wc = prep
    m = als.shape[0]
    nb = min(nb, bsz)
    seq = m // bsz
    nc = seq // CHUNK
    per_batch = lambda a: a.reshape((bsz, seq) + a.shape[1:])
    wide = pl.BlockSpec((nb, CHUNK, GROUP_W), lambda b, c: (b, c, 0))
    out = pl.pallas_call(
        _rwkv_scan_kernel,
        grid=(bsz // nb, nc),
        in_specs=[wide, wide, wide,
                  pl.BlockSpec((nb, CHUNK, 4 * GROUP_W), lambda b, c: (b, c, 0)),
                  wide, wide, wide, wide,
                  pl.BlockSpec((nb, 1, 8, GROUP_W), lambda b, c: (b, c, 0, 0)),
                  pl.BlockSpec(g.shape, lambda b, c: (0, 0))],
        out_specs=wide,
        out_shape=jax.ShapeDtypeStruct((bsz, seq, GROUP_W), BF16),
        scratch_shapes=[pltpu.VMEM((nb, 2, LANES, LANES), F32)],
        compiler_params=_cparams(("parallel", "arbitrary")),
        name="rwkv_scan",
    )(per_batch(als), per_batch(rs), per_batch(bh), per_batch(tg), per_batch(yind), per_batch(bonus),
      per_batch(gate), per_batch(sconst), wc.reshape(bsz, nc, 8, GROUP_W), g)
    return out.reshape(m, GROUP_W)


def _row(v, width=None, offset=0):
    v = v.astype(F32).reshape(-1)
    width = v.shape[0] if width is None else width
    return jnp.zeros((1, width), F32).at[0, offset:offset + v.shape[0]].set(v)


def _w_in_segments():
    gw = GROUP_W
    b0 = 3 * gw
    c0 = b0 + 4 * gw + GROUP_HEADS
    d0 = c0 + 5 * gw + 2 * GROUP_HEADS
    m0 = d0 + 4 * gw + DECAY_RANK + A_RANK
    wide = [0, gw, 2 * gw,
            b0, b0 + gw, b0 + 2 * gw, b0 + 3 * gw + GROUP_HEADS,
            c0, c0 + gw, c0 + 2 * gw, c0 + 3 * gw + 2 * GROUP_HEADS, c0 + 4 * gw + 2 * GROUP_HEADS,
            d0, d0 + gw + DECAY_RANK, d0 + 2 * gw + DECAY_RANK, d0 + 3 * gw + DECAY_RANK + A_RANK,
            m0, m0 + gw]
    narrow = [(b0 + 3 * gw, GROUP_HEADS, SM_BF), (c0 + 3 * gw, 2 * GROUP_HEADS, SM_CI),
              (d0 + gw, DECAY_RANK, SM_WLO), (d0 + 3 * gw + DECAY_RANK, A_RANK, SM_ALO)]
    return wide, narrow, m0 + 2 * gw


def _wlayout_kernel(w_ref, o_ref):
    wide, narrow, _ = _w_in_segments()
    rows = w_ref.shape[0]
    for seg, src in enumerate(wide):
        for half in range(GROUP_W // LANES):
            s = src + half * LANES
            base, off = (s // LANES) * LANES, s % LANES
            win = w_ref[:, base:base + 2 * LANES]
            blk = win[:, :LANES] if off == 0 else pltpu.roll(win, 2 * LANES - off, axis=1)[:, :LANES]
            d = seg * GROUP_W + half * LANES
            o_ref[:, d:d + LANES] = blk.astype(o_ref.dtype)
    lane = _iota((rows, LANES), 1)
    small = jnp.zeros((rows, LANES), F32)
    for src, width, dst in narrow:
        base, off = (src // LANES) * LANES, src % LANES
        win = w_ref[:, base:base + LANES]
        if dst != off:
            win = pltpu.roll(win, (dst - off) % LANES, axis=1)
        small = jnp.where((lane >= dst) & (lane < dst + width), win, small)
    o_ref[:, N_WIDE:] = small.astype(o_ref.dtype)


def _layout_w_in(w_in, rows=256):
    d, n_src = w_in.shape
    assert n_src == _w_in_segments()[2]
    w_pad = jnp.pad(w_in, ((0, 0), (0, NP + LANES - n_src)))
    return pl.pallas_call(
        _wlayout_kernel,
        grid=(d // rows,),
        in_specs=[pl.BlockSpec((rows, NP + LANES), lambda i: (i, 0))],
        out_specs=pl.BlockSpec((rows, NP), lambda i: (i, 0)),
        out_shape=jax.ShapeDtypeStruct((d, NP), BF16),
        compiler_params=_cparams(("parallel",)),
        name="wlayout",
    )(w_pad)


def _layout_mu(mu):
    gw = GROUP_W
    o_w, o_k, o_v, o_a, o_z = gw, gw + DECAY_RANK, 2 * gw + DECAY_RANK, 3 * gw + DECAY_RANK, 3 * gw + DECAY_RANK + A_RANK
    mu_s = jnp.zeros((1, LANES), F32)
    mu_s = mu_s.at[0, SM_WLO:SM_WLO + DECAY_RANK].set(mu[o_w:o_w + DECAY_RANK])
    mu_s = mu_s.at[0, SM_ALO:SM_ALO + A_RANK].set(mu[o_a:o_a + A_RANK])
    return dict(mu_r=_row(mu[0:gw]), mu_k=_row(mu[o_k:o_k + gw]), mu_v=_row(mu[o_v:o_v + gw]),
                mu_z=_row(mu[o_z:o_z + gw]), mu_s=mu_s)


def _pad_rows(w, offset):
    return jnp.zeros((LANES, w.shape[1]), F32).at[offset:offset + w.shape[0]].set(w).astype(BF16)


def _layer(x2, mem2, bsz, n_mem, norm_g, w_in, w_out, sgu_norm_g, sgu_w, sgu_b, fox_q_g, fox_k_g, fox_f_b,
           mlstm_conv_w, mlstm_conv_b, mlstm_i_b, mlstm_f_b, mlstm_out_g,
           rwkv_mu, rwkv_w0, rwkv_w2, rwkv_a0, rwkv_a2, rwkv_k_k, rwkv_k_a, rwkv_r_k, rwkv_ln_g,
           mem_norm_g, mem_w_kv, mem_q_g, mem_k_g):
    tile_h = lambda g: _row(jnp.tile(g, GROUP_HEADS))
    proj = _inproj(x2, _row(norm_g), _layout_w_in(w_in))

    ya = _sgu(proj, _row(sgu_norm_g), sgu_w, jnp.repeat(sgu_b.T, HEAD_DIM, axis=1))

    kaug, qaug_t, v_t = _foxprep(proj, tile_h(fox_q_g), tile_h(fox_k_g), _row(fox_f_b, LANES, SM_BF), bsz)
    yb = _fox(proj, kaug, qaug_t, v_t, bsz)

    cw = jnp.zeros((8, 2 * GROUP_W), F32).at[:CONV_K].set(mlstm_conv_w)
    yc = _mlstm(proj, cw, _row(mlstm_conv_b), _row(mlstm_i_b, LANES, SM_CI), _row(mlstm_f_b, LANES, SM_CF),
                _row(mlstm_out_g), bsz)

    prm = _layout_mu(rwkv_mu)
    prm.update(w0=_row(rwkv_w0), w2p=_pad_rows(rwkv_w2, SM_WLO), a0=_row(rwkv_a0), a2p=_pad_rows(rwkv_a2, SM_ALO),
               k_k=_row(rwkv_k_k), k_a=_row(rwkv_k_a), r_k=_row(rwkv_r_k))
    yd = _rwkv_scan(_rwkv_prep(proj, prm, bsz), _row(rwkv_ln_g), bsz)

    mk, mv = _memkv(mem2, _row(mem_norm_g), mem_w_kv.astype(BF16), tile_h(mem_k_g), n_mem)
    ym = _memattn(proj, mk, mv, tile_h(mem_q_g), bsz, n_mem)

    return _outproj([ya, yb, yc, yd, ym], w_out.astype(BF16), x2)


def kernel(x, mem, norm_g, w_in, w_out, sgu_norm_g, sgu_w, sgu_b, fox_q_g, fox_k_g, fox_f_b, mlstm_conv_w, mlstm_conv_b, mlstm_i_b, mlstm_f_b, mlstm_out_g, rwkv_mu, rwkv_w0, rwkv_w2, rwkv_a0, rwkv_a2, rwkv_k_k, rwkv_k_a, rwkv_r_k, rwkv_ln_g, mem_norm_g, mem_w_kv, mem_q_g, mem_k_g):
    bsz, seq, d = x.shape
    n_mem = mem.shape[1]
    stacked = (norm_g, w_in, w_out, sgu_norm_g, sgu_w, sgu_b, fox_q_g, fox_k_g, fox_f_b,
               mlstm_conv_w, mlstm_conv_b, mlstm_i_b, mlstm_f_b, mlstm_out_g,
               rwkv_mu, rwkv_w0, rwkv_w2, rwkv_a0, rwkv_a2, rwkv_k_k, rwkv_k_a, rwkv_r_k, rwkv_ln_g,
               mem_norm_g, mem_w_kv, mem_q_g, mem_k_g)
    x2 = x.reshape(bsz * seq, d)
    mem2 = mem.reshape(bsz * n_mem, d)
    for layer in range(norm_g.shape[0]):
        x2 = _layer(x2, mem2, bsz, n_mem, *[p[layer] for p in stacked])
    return x2.reshape(bsz, seq, d)
```

```python
import functools

import jax
import jax.numpy as jnp
from jax import lax
from jax.experimental import pallas as pl
from jax.experimental.pallas import tpu as pltpu

F32 = jnp.float32
BF16 = jnp.bfloat16

HEAD_DIM = 64
GROUP_HEADS = 4
GROUP_W = GROUP_HEADS * HEAD_DIM
LANES = 128
CHUNK = 128
CONV_K = 4
EPS = 1e-6
SCALE = HEAD_DIM ** -0.5

CB = dict(a_u=0, a_v=1, a_z=2, b_q=3, b_k=4, b_v=5, b_z=6, c_q=7, c_k=8, c_v=9, c_o=10, c_z=11,
          d_r=12, d_k=13, d_v=14, d_z=15, m_q=16, m_z=17)
N_WIDE = 18 * GROUP_W
SMALL_CB = N_WIDE // LANES
NP = N_WIDE + LANES
SM_BF, SM_CI, SM_CF, SM_WLO, SM_ALO = 0, 4, 8, 16, 32
DECAY_RANK = 16
A_RANK = 16

VMEM_LIMIT = 48 * 1024 * 1024

NN = (((1,), (0,)), ((), ()))
NT = (((1,), (1,)), ((), ()))
TN = (((0,), (0,)), ((), ()))


def _mm(a, b, dims=NN):
    return lax.dot_general(a.astype(BF16), b.astype(BF16), dims, preferred_element_type=F32)


def _split(x, n):
    if x.dtype == BF16 or n == 1:
        return [x.astype(BF16)]
    parts, r = [], x
    for _ in range(n):
        p = r.astype(BF16)
        parts.append(p)
        r = r - p.astype(F32)
    return parts


def _mm_x(a, b, dims=NN, pa=1, pb=1):
    pa_list, pb_list = _split(a, pa), _split(b, pb)
    order = max(len(pa_list), len(pb_list))
    acc = None
    for i, ai in enumerate(pa_list):
        for j, bj in enumerate(pb_list):
            if i + j < order:
                t = lax.dot_general(ai, bj, dims, preferred_element_type=F32)
                acc = t if acc is None else acc + t
    return acc


def _iota(shape, axis):
    return lax.broadcasted_iota(jnp.int32, shape, axis)


def _tril(n, strict=False):
    r, c = _iota((n, n), 0), _iota((n, n), 1)
    return (c < r) if strict else (c <= r)


def _head_sum(x):
    w = x.shape[-1]
    bd = (_iota((w, w), 0) // HEAD_DIM == _iota((w, w), 1) // HEAD_DIM).astype(BF16)
    return _mm_x(x, bd, pa=3)


def _head_rms(x, g):
    ms = _head_sum(x * x) * (1.0 / HEAD_DIM)
    return x * lax.rsqrt(ms + EPS) * g


def _lo_mask(shape):
    return _iota(shape, len(shape) - 1) % LANES < HEAD_DIM


def _cparams(sem, vmem_limit=VMEM_LIMIT):
    return pltpu.CompilerParams(dimension_semantics=sem, vmem_limit_bytes=vmem_limit)


def _inproj_kernel(x_ref, g_ref, w_ref, o_ref, *, col_chunk):
    x = x_ref[...]
    ms = jnp.mean(x * x, axis=-1, keepdims=True)
    h = (x * lax.rsqrt(ms + EPS) * g_ref[...]).astype(BF16)
    n = o_ref.shape[1]
    for c0 in range(0, n, col_chunk):
        c1 = min(c0 + col_chunk, n)
        o_ref[:, c0:c1] = jnp.dot(h, w_ref[:, c0:c1], preferred_element_type=F32)


def _inproj(x2, g, w_bf16, tm=512):
    m, d = x2.shape
    n = w_bf16.shape[1]
    return pl.pallas_call(
        functools.partial(_inproj_kernel, col_chunk=512),
        grid=(m // tm,),
        in_specs=[pl.BlockSpec((tm, d), lambda i: (i, 0)),
                  pl.BlockSpec((1, d), lambda i: (0, 0)),
                  pl.BlockSpec((d, n), lambda i: (0, 0))],
        out_specs=pl.BlockSpec((tm, n), lambda i: (i, 0)),
        out_shape=jax.ShapeDtypeStruct((m, n), F32),
        compiler_params=_cparams(("parallel",), 2 * (2 * d * n + 4 * tm * n + 4 * tm * d) + (8 << 20)),
        name="inproj",
    )(x2, g, w_bf16)


def _outproj_kernel(ya, yb, yc, yd, ym, w_ref, x_ref, o_ref):
    mixed = jnp.concatenate([ya[...], yb[...], yc[...], yd[...], ym[...]], axis=1)
    o_ref[...] = x_ref[...] + jnp.dot(mixed, w_ref[...], preferred_element_type=F32)


def _outproj(ys, w_bf16, x2, tm=1024):
    m, d = x2.shape
    yspec = pl.BlockSpec((tm, GROUP_W), lambda i: (i, 0))
    return pl.pallas_call(
        _outproj_kernel,
        grid=(m // tm,),
        in_specs=[yspec] * 5 + [pl.BlockSpec(w_bf16.shape, lambda i: (0, 0)),
                                pl.BlockSpec((tm, d), lambda i: (i, 0))],
        out_specs=pl.BlockSpec((tm, d), lambda i: (i, 0)),
        out_shape=jax.ShapeDtypeStruct((m, d), F32),
        compiler_params=_cparams(("parallel",)),
        name="outproj",
    )(*ys, w_bf16, x2)


def _sgu_kernel(u_ref, v_ref, z_ref, g_ref, w_ref, bias_ref, o_ref):
    t = u_ref.shape[0]
    u = jax.nn.gelu(u_ref[...])
    vn = _head_rms(jax.nn.gelu(v_ref[...]), g_ref[...])
    gate = jax.nn.silu(z_ref[...])
    tril = _tril(CHUNK)
    lo = _lo_mask((CHUNK, LANES))
    ws = [jnp.where(tril, w_ref[g], 0.0).astype(BF16) for g in range(GROUP_HEADS)]
    bias = bias_ref[...]
    for c in range(t // CHUNK):
        sl = slice(c * CHUNK, (c + 1) * CHUNK)
        tiles = []
        for p in range(2):
            vt = vn[sl, p * LANES:(p + 1) * LANES]
            tiles.append(_mm(ws[2 * p], jnp.where(lo, vt, 0.0)) + _mm(ws[2 * p + 1], jnp.where(lo, 0.0, vt)))
        mixed = jnp.concatenate(tiles, axis=1) + bias
        o_ref[sl, :] = (u[sl] * mixed * gate[sl]).astype(o_ref.dtype)


def _sgu(proj, g, w, bias_tile, t=512):
    m = proj.shape[0]
    col = lambda cb: pl.BlockSpec((t, GROUP_W), lambda i, cb=cb: (i, cb))
    full = lambda a: pl.BlockSpec(a.shape, lambda i, n=a.ndim: (0,) * n)
    return pl.pallas_call(
        _sgu_kernel,
        grid=(m // t,),
        in_specs=[col(CB["a_u"]), col(CB["a_v"]), col(CB["a_z"]), full(g), full(w), full(bias_tile)],
        out_specs=pl.BlockSpec((t, GROUP_W), lambda i: (i, 0)),
        out_shape=jax.ShapeDtypeStruct((m, GROUP_W), BF16),
        compiler_params=_cparams(("parallel",)),
        name="sgu",
    )(proj, proj, proj, g, w, bias_tile)


def _memkv_kernel(mem_ref, g_ref, w_ref, kg_ref, k_ref, v_ref):
    x = mem_ref[...]
    ms = jnp.mean(x * x, axis=-1, keepdims=True)
    h = (x * lax.rsqrt(ms + EPS) * g_ref[...]).astype(BF16)
    kv = jnp.dot(h, w_ref[...], preferred_element_type=F32)
    k_ref[...] = _head_rms(kv[:, :GROUP_W], kg_ref[...]).astype(BF16)
    v_ref[...] = kv[:, GROUP_W:].astype(BF16)


def _memkv(mem2, g, w_bf16, kg, n_mem):
    m, d = mem2.shape
    full = lambda a: pl.BlockSpec(a.shape, lambda i, n=a.ndim: (0,) * n)
    out = pl.BlockSpec((n_mem, GROUP_W), lambda i: (i, 0))
    return pl.pallas_call(
        _memkv_kernel,
        grid=(m // n_mem,),
        in_specs=[pl.BlockSpec((n_mem, d), lambda i: (i, 0)), full(g), full(w_bf16), full(kg)],
        out_specs=[out, out],
        out_shape=[jax.ShapeDtypeStruct((m, GROUP_W), BF16)] * 2,
        compiler_params=_cparams(("parallel",)),
        name="memkv",
    )(mem2, g, w_bf16, kg)


def _memattn_kernel(q_ref, z_ref, k_ref, v_ref, qg_ref, o_ref):
    q = (_head_rms(q_ref[...], qg_ref[...]) * SCALE).astype(BF16)
    k = k_ref[...]
    v = v_ref[...]
    t = q.shape[0]
    lo_q = _lo_mask((t, LANES))
    lo_v = _lo_mask((v.shape[0], LANES))
    zero = jnp.zeros((), BF16)
    heads = range(GROUP_HEADS)
    sls = [slice((h // 2) * LANES, (h // 2 + 1) * LANES) for h in heads]
    s = [_mm(jnp.where(lo_q if h % 2 == 0 else ~lo_q, q[:, sls[h]], zero), k[:, sls[h]], NT) for h in heads]
    e = [jnp.exp(s[h] - jnp.max(s[h], axis=-1, keepdims=True)) for h in heads]
    pr = [e[h] / jnp.sum(e[h], axis=-1, keepdims=True) for h in heads]
    pv = [_mm(pr[h], jnp.where(lo_v if h % 2 == 0 else ~lo_v, v[:, sls[h]], zero)) for h in heads]
    y = jnp.concatenate([pv[0] + pv[1], pv[2] + pv[3]], axis=1)
    o_ref[...] = (y * jax.nn.silu(z_ref[...])).astype(o_ref.dtype)


def _memattn(proj, kn, vb, qg, bsz, n_mem, t=512):
    m = proj.shape[0]
    nt = m // bsz // t
    col = lambda cb: pl.BlockSpec((t, GROUP_W), lambda b, i, cb=cb: (b * nt + i, cb))
    kv = pl.BlockSpec((n_mem, GROUP_W), lambda b, i: (b, 0))
    return pl.pallas_call(
        _memattn_kernel,
        grid=(bsz, nt),
        in_specs=[col(CB["m_q"]), col(CB["m_z"]), kv, kv, pl.BlockSpec(qg.shape, lambda b, i: (0, 0))],
        out_specs=pl.BlockSpec((t, GROUP_W), lambda b, i: (b * nt + i, 0)),
        out_shape=jax.ShapeDtypeStruct((m, GROUP_W), BF16),
        compiler_params=_cparams(("parallel", "parallel")),
        name="memattn",
    )(proj, proj, kn, vb, qg)


LOG2E = 1.4426950408889634
F_TERMS = 3
V_ROWS = 80


def _bf16_terms(x, n):
    terms, r = [], x
    for _ in range(n):
        p = r.astype(BF16).astype(F32)
        terms.append(p)
        r = r - p
    return terms


def _foxprep_kernel(q_ref, k_ref, v_ref, sm_ref, qg_ref, kg_ref, fb_ref,
                    kaug_ref, qaug_t_ref, v_t_ref, carry_ref):
    @pl.when(pl.program_id(1) == 0)
    def _():
        carry_ref[...] = jnp.zeros_like(carry_ref)

    t = q_ref.shape[0]
    qn = _head_rms(q_ref[...], qg_ref[...]) * (SCALE * LOG2E)
    kn = _head_rms(k_ref[...], kg_ref[...])
    lf = jax.nn.log_sigmoid(sm_ref[...] + fb_ref[...])
    cs = _mm_x(_tril(t).astype(BF16), lf, pb=3) + carry_ref[0:1, :]
    carry_ref[...] = jnp.broadcast_to(cs[t - 1:t, :], carry_ref.shape)
    f2 = cs * LOG2E

    lane = _iota((t, LANES), 1)
    q_tiles = []
    for h in range(GROUP_HEADS):
        sl = slice((h // 2) * LANES, (h // 2 + 1) * LANES)
        q_t, k_t = qn[:, sl], kn[:, sl]
        if h % 2 == 1:
            q_t = pltpu.roll(q_t, HEAD_DIM, axis=1)
            k_t = pltpu.roll(k_t, HEAD_DIM, axis=1)
        terms = _bf16_terms(f2[:, SM_BF + h:SM_BF + h + 1], F_TERMS)
        qa = jnp.where(lane < HEAD_DIM + 2 * F_TERMS, 1.0, 0.0)
        ka = qa
        for n, term in enumerate(terms):
            qa = jnp.where(lane == HEAD_DIM + n, term, qa)
            ka = jnp.where(lane == HEAD_DIM + F_TERMS + n, -term, ka)
        qa = jnp.where(lane < HEAD_DIM, q_t, qa)
        ka = jnp.where(lane < HEAD_DIM, k_t, ka)
        kaug_ref[:, h * LANES:(h + 1) * LANES] = ka.astype(BF16)
        q_tiles.append(qa)
    qaug_t_ref[0] = jnp.concatenate(q_tiles, axis=1).T.astype(BF16)
    v_t = v_ref[...].T
    pad_rows = jnp.where(_iota((V_ROWS - HEAD_DIM, t), 0) == 0, 1.0, 0.0)
    v_t_ref[0] = jnp.concatenate(
        [x for h in range(GROUP_HEADS) for x in (v_t[h * HEAD_DIM:(h + 1) * HEAD_DIM], pad_rows)],
        axis=0).astype(BF16)


def _foxprep(proj, qg, kg, fb_row, bsz, t=512):
    m = proj.shape[0]
    seq = m // bsz
    nt = seq // t
    col = lambda cb: pl.BlockSpec((t, GROUP_W), lambda b, i, cb=cb: (b * nt + i, cb))
    row = lambda a: pl.BlockSpec(a.shape, lambda b, i: (0, 0))
    aug_w = GROUP_HEADS * LANES
    return pl.pallas_call(
        _foxprep_kernel,
        grid=(bsz, nt),
        in_specs=[col(CB["b_q"]), col(CB["b_k"]), col(CB["b_v"]),
                  pl.BlockSpec((t, LANES), lambda b, i: (b * nt + i, SMALL_CB)),
                  row(qg), row(kg), row(fb_row)],
        out_specs=[pl.BlockSpec((t, aug_w), lambda b, i: (b * nt + i, 0)),
                   pl.BlockSpec((1, aug_w, t), lambda b, i: (b, 0, i)),
                   pl.BlockSpec((1, GROUP_HEADS * V_ROWS, t), lambda b, i: (b, 0, i))],
        out_shape=[jax.ShapeDtypeStruct((m, aug_w), BF16),
                   jax.ShapeDtypeStruct((bsz, aug_w, seq), BF16),
                   jax.ShapeDtypeStruct((bsz, GROUP_HEADS * V_ROWS, seq), BF16)],
        scratch_shapes=[pltpu.VMEM((8, LANES), F32)],
        compiler_params=_cparams(("parallel", "arbitrary")),
        name="foxprep",
    )(proj, proj, proj, proj, qg, kg, fb_row)


def _fox_kernel(q_ref, k_ref, v_ref, z_ref, o_ref, m_scr, acc_scr, sa_scr, sb_scr, mxa_scr, mxb_scr,
                *, tq, tk):
    i = pl.program_id(2)
    m_scr[...] = jnp.full_like(m_scr, -jnp.inf)
    acc_scr[...] = jnp.zeros_like(acc_scr)

    def scores(j, s_scr, mx_scr, q0=0):
        k0 = pl.multiple_of(j * tk, tk)
        for hh in range(2):
            k_blk = k_ref[pl.ds(k0, tk), hh * LANES:(hh + 1) * LANES]
            s_t = jnp.dot(k_blk, q_ref[0, hh * LANES:(hh + 1) * LANES, q0:],
                          preferred_element_type=F32)
            s_scr[hh, :, q0:] = s_t
            mx_scr[hh, :, q0:] = jnp.broadcast_to(jnp.max(s_t, axis=0, keepdims=True), (8, tq - q0))

    def softmax_pv(j, s_scr, mx_scr, masked=False, q0=0):
        k0 = pl.multiple_of(j * tk, tk)
        nq_ = tq - q0
        for hh in range(2):
            s_t = s_scr[hh, :, q0:]
            mx = mx_scr[hh, 0:1, q0:]
            if masked:
                visible = (k0 + _iota((tk, nq_), 0)) <= (i * tq + q0 + _iota((tk, nq_), 1))
                s_t = jnp.where(visible, s_t, -jnp.inf)
                mx = jnp.max(s_t, axis=0, keepdims=True)
            m_old = m_scr[hh, 0:1, q0:]
            m_new = jnp.maximum(m_old, mx)
            alpha = jnp.exp2(m_old - m_new)
            p_t = jnp.exp2(s_t - m_new)
            m_scr[hh, :, q0:] = jnp.broadcast_to(m_new, (8, nq_))
            v_blk = v_ref[0, hh * V_ROWS:(hh + 1) * V_ROWS, pl.ds(k0, tk)]
            acc_scr[hh, :, q0:] = (acc_scr[hh, :, q0:] * alpha
                                   + jnp.dot(v_blk, p_t.astype(BF16), preferred_element_type=F32))

    assert tq == 2 * tk
    n_full = 2 * i
    scores(0, sa_scr, mxa_scr)

    def two_blocks(jj, carry):
        scores(2 * jj + 1, sb_scr, mxb_scr)
        softmax_pv(2 * jj, sa_scr, mxa_scr)
        scores(2 * jj + 2, sa_scr, mxa_scr)
        softmax_pv(2 * jj + 1, sb_scr, mxb_scr)
        return carry

    lax.fori_loop(0, i, two_blocks, 0)
    scores(n_full + 1, sb_scr, mxb_scr, q0=tk)
    softmax_pv(n_full, sa_scr, mxa_scr, masked=True)
    softmax_pv(n_full + 1, sb_scr, mxb_scr, masked=True, q0=tk)

    y_t = jnp.concatenate([acc_scr[hh][:HEAD_DIM] / acc_scr[hh][HEAD_DIM:HEAD_DIM + 1] for hh in range(2)],
                          axis=0)
    o_ref[...] = (y_t.T * jax.nn.silu(z_ref[...])).astype(o_ref.dtype)


def _fox(proj, kaug, qaug_t, v_t, bsz, tq=512, tk=256):
    m = proj.shape[0]
    seq = m // bsz
    nq = seq // tq
    s_buf = pltpu.VMEM((2, tk, tq), F32)
    mx_buf = pltpu.VMEM((2, 8, tq), F32)
    return pl.pallas_call(
        functools.partial(_fox_kernel, tq=tq, tk=tk),
        grid=(bsz, 2, nq),
        in_specs=[pl.BlockSpec((1, 2 * LANES, tq), lambda b, p, i: (b, p, i)),
                  pl.BlockSpec((seq, 2 * LANES), lambda b, p, i: (b, p)),
                  pl.BlockSpec((1, 2 * V_ROWS, seq), lambda b, p, i: (b, p, 0)),
                  pl.BlockSpec((tq, LANES), lambda b, p, i: (b * nq + i, 2 * CB["b_z"] + p))],
        out_specs=pl.BlockSpec((tq, LANES), lambda b, p, i: (b * nq + i, p)),
        out_shape=jax.ShapeDtypeStruct((m, GROUP_W), BF16),
        scratch_shapes=[pltpu.VMEM((2, 8, tq), F32), pltpu.VMEM((2, V_ROWS, tq), F32),
                        s_buf, s_buf, mx_buf, mx_buf],
        compiler_params=_cparams(("parallel", "parallel", "arbitrary")),
        name="fox",
    )(qaug_t, kaug, v_t, proj)


def _mlstm_kernel(q_ref, k_ref, v_ref, og_ref, z_ref, sm_ref, cw_ref, cb_ref, ib_ref, fb_ref, g_ref,
                  out_ref, xbuf, c_scr, n_scr, m_scr):
    t = CHUNK
    nb = q_ref.shape[0]

    @pl.when(pl.program_id(1) == 0)
    def _():
        xbuf[:, 0:8, :] = jnp.zeros((nb, 8, 2 * GROUP_W), F32)
        c_scr[...] = jnp.zeros_like(c_scr)
        n_scr[...] = jnp.zeros_like(n_scr)
        m_scr[...] = jnp.zeros_like(m_scr)

    tril = _tril(t)
    tril_bf = tril.astype(BF16)
    lo = _lo_mask((t, LANES))
    lo_row = _lo_mask((1, LANES))
    blockdiag = (_iota((LANES, LANES), 0) < HEAD_DIM) == (_iota((LANES, LANES), 1) < HEAD_DIM)
    lane_row = _iota((1, LANES), 1)
    sls = [slice(p * LANES, (p + 1) * LANES) for p in range(2)]

    qs, ks, vs, lis, bcums, b_ts, i_ts, m_rows = [], [], [], [], [], [], [], []
    for b in range(nb):
        xbuf[b, 8:8 + t, 0:GROUP_W] = q_ref[b]
        xbuf[b, 8:8 + t, GROUP_W:] = k_ref[b]
        conv = cb_ref[...]
        for jj in range(CONV_K):
            conv = conv + cw_ref[jj:jj + 1, :] * xbuf[b, 8 - (CONV_K - 1) + jj:8 - (CONV_K - 1) + jj + t, :]
        xbuf[b, 0:8, :] = xbuf[b, t:t + 8, :]
        qk_act = jax.nn.silu(conv)
        qs.append(qk_act[:, :GROUP_W])
        ks.append(qk_act[:, GROUP_W:] * SCALE)
        vs.append(v_ref[b])
        sm = sm_ref[b]
        li = sm + ib_ref[...]
        lf = jax.nn.log_sigmoid(sm + fb_ref[...])
        bcum = _mm_x(tril_bf, lf, pb=3)
        lis.append(li)
        bcums.append(bcum)
        b_ts.append(bcum.T)
        i_ts.append(li.T)
        m_rows.append(m_scr[b, 0:1, :])

    units = [(b, h) for b in range(nb) for h in range(GROUP_HEADS)]
    pairs = [(b, p) for b in range(nb) for p in range(2)]
    idx = range(len(units))
    pair_of = [2 * b + h // 2 for b, h in units]
    masks = [lo if h % 2 == 0 else ~lo for _, h in units]
    qp = [qs[b][:, sls[p]] for b, p in pairs]
    kp = [ks[b][:, sls[p]] for b, p in pairs]
    vp = [vs[b][:, sls[p]] for b, p in pairs]
    c_ps = [c_scr[b, p] for b, p in pairs]
    n_ps = [n_scr[b, p][0:1, :] for b, p in pairs]
    qk = [_mm(jnp.where(masks[u], qp[pair_of[u]], 0.0), kp[pair_of[u]], NT) for u in idx]
    q_c = [_mm(qp[i], c_ps[i]) for i in range(len(pairs))]
    q_n = [qp[i] * n_ps[i] for i in range(len(pairs))]
    b_c = [bcums[b][:, SM_CF + h:SM_CF + h + 1] for b, h in units]
    b_r = [b_ts[b][SM_CF + h:SM_CF + h + 1, :] for b, h in units]
    i_c = [lis[b][:, SM_CI + h:SM_CI + h + 1] for b, h in units]
    i_r = [i_ts[b][SM_CI + h:SM_CI + h + 1, :] for b, h in units]
    m_st = [m_rows[b][:, h:h + 1] for b, h in units]
    log_d = [jnp.where(tril, b_c[u] - b_r[u] + i_r[u], -jnp.inf) for u in idx]
    inter = [b_c[u] + m_st[u] for u in idx]
    m_t = [jnp.maximum(jnp.max(log_d[u], axis=-1, keepdims=True), inter[u]) for u in idx]
    s = [qk[u] * jnp.exp(log_d[u] - m_t[u]) for u in idx]
    w_inter = [jnp.exp(inter[u] - m_t[u]) for u in idx]
    sv = [_mm(s[u], jnp.where(masks[u], vp[pair_of[u]], 0.0)) for u in idx]
    num = [sv[u] + w_inter[u] * jnp.where(masks[u], q_c[pair_of[u]], 0.0) for u in idx]
    den = [jnp.sum(s[u], axis=-1, keepdims=True)
           + w_inter[u] * jnp.sum(jnp.where(masks[u], q_n[pair_of[u]], 0.0), axis=-1, keepdims=True) for u in idx]
    hv = [num[u] / jnp.maximum(jnp.abs(den[u]), jnp.exp(-m_t[u])) for u in idx]
    g = [b_c[u][t - 1:t, :] for u in idx]
    m_new = [jnp.maximum(g[u] + m_st[u], jnp.max(g[u] - b_r[u] + i_r[u], axis=-1, keepdims=True)) for u in idx]
    w_col = [jnp.exp(g[u] - b_c[u] + i_c[u] - m_new[u]) for u in idx]
    cd = [jnp.exp(g[u] + m_st[u] - m_new[u]) for u in idx]
    first = lambda b, p: GROUP_HEADS * b + 2 * p
    kw = [kp[i] * jnp.where(lo, w_col[first(b, p)], w_col[first(b, p) + 1]) for i, (b, p) in enumerate(pairs)]
    cd_row = [jnp.where(lo_row, cd[first(b, p)], cd[first(b, p) + 1]) for b, p in pairs]
    upd = [_mm(kw[i], vp[i], TN) for i in range(len(pairs))]
    for i, (b, p) in enumerate(pairs):
        c_scr[b, p] = cd_row[i] * c_ps[i] + jnp.where(blockdiag, upd[i], 0.0)
        n_new = cd_row[i] * n_ps[i] + jnp.sum(kw[i], axis=0, keepdims=True)
        n_scr[b, p] = jnp.broadcast_to(n_new, (8, LANES))
    for b in range(nb):
        m_row_new = m_rows[b]
        for h in range(GROUP_HEADS):
            m_row_new = jnp.where(lane_row == h, m_new[GROUP_HEADS * b + h], m_row_new)
        m_scr[b] = jnp.broadcast_to(m_row_new, (8, LANES))
        h_tiles = [hv[first(b, p)] + hv[first(b, p) + 1] for p in range(2)]
        hcat = jax.nn.sigmoid(og_ref[b]) * jnp.concatenate(h_tiles, axis=1)
        y = _head_rms(hcat, g_ref[...])
        out_ref[b] = (y * jax.nn.silu(z_ref[b])).astype(out_ref.dtype)


def _mlstm(proj, cw, cb, ib_row, fb_row, g, bsz, nb=4):
    m = proj.shape[0]
    seq = m // bsz
    nc = seq // CHUNK
    nb = min(nb, bsz)
    proj3 = proj.reshape(bsz, seq, proj.shape[1])
    col = lambda cb_: pl.BlockSpec((nb, CHUNK, GROUP_W), lambda b, c, cb_=cb_: (b, c, cb_))
    full = lambda a: pl.BlockSpec(a.shape, lambda b, c, n=a.ndim: (0,) * n)
    out = pl.pallas_call(
        _mlstm_kernel,
        grid=(bsz // nb, nc),
        in_specs=[col(CB["c_q"]), col(CB["c_k"]), col(CB["c_v"]), col(CB["c_o"]), col(CB["c_z"]),
                  pl.BlockSpec((nb, CHUNK, LANES), lambda b, c: (b, c, SMALL_CB)),
                  full(cw), full(cb), full(ib_row), full(fb_row), full(g)],
        out_specs=pl.BlockSpec((nb, CHUNK, GROUP_W), lambda b, c: (b, c, 0)),
        out_shape=jax.ShapeDtypeStruct((bsz, seq, GROUP_W), BF16),
        scratch_shapes=[pltpu.VMEM((nb, CHUNK + 8, 2 * GROUP_W), F32), pltpu.VMEM((nb, 2, LANES, LANES), F32),
                        pltpu.VMEM((nb, 2, 8, LANES), F32), pltpu.VMEM((nb, 8, LANES), F32)],
        compiler_params=_cparams(("parallel", "arbitrary")),
        name="mlstm",
    )(proj3, proj3, proj3, proj3, proj3, proj3, cw, cb, ib_row, fb_row, g)
    return out.reshape(m, GROUP_W)


PASSES_AA = 1
PASSES_INV = 1
PASSES_STATE = 1


def _mmp(a, b, dims=NN, passes=1):
    return _mm_x(a, b, dims, pa=passes, pb=passes)


def _tri_inverse(mats):
    n = mats[0].shape[0]
    r, c = _iota((n, n), 0), _iota((n, n), 1)
    same = lambda b: (r // b) == (c // b)
    mm = lambda x, y: _mmp(x, y, passes=PASSES_INV)
    eye = jnp.where(r == c, 1.0, 0.0)
    a8 = [jnp.where(same(8), a, 0.0) for a in mats]
    ts = [eye + x for x in a8]
    ps = [mm(x, x) for x in a8]
    ts = [t + mm(t, p) for t, p in zip(ts, ps)]
    ps = [mm(p, p) for p in ps]
    ts = [t + mm(t, p) for t, p in zip(ts, ps)]
    b = 8
    while b < n:
        off = same(2 * b) & ~same(b)
        inner = [mm(jnp.where(off, a, 0.0), t) for a, t in zip(mats, ts)]
        ts = [t + mm(t, x) for t, x in zip(ts, inner)]
        b *= 2
    return ts


def _rwkv_prep_kernel(r_ref, k_ref, v_ref, z_ref, sm_ref, pr_ref, pk_ref, pv_ref, pz_ref, psm_ref,
                      mur_ref, muk_ref, muv_ref, muz_ref, mus_ref,
                      w0_ref, w2_ref, a0_ref, a2_ref, kk_ref, ka_ref, rk_ref,
                      als_ref, rs_ref, bh_ref, tg_ref, uind_ref, yind_ref, bonus_ref, gate_ref,
                      sconst_ref, wc_ref, *, steps_per_seq):
    t = CHUNK
    first = (pl.program_id(0) % steps_per_seq) == 0
    row0 = _iota((r_ref.shape[0], 1), 0) == 0

    def shifted(x_ref, p_ref, mu_ref):
        x = x_ref[...]
        prev_row = jnp.where(first, 0.0, p_ref[7:8, :])
        prev = jnp.where(row0, prev_row, pltpu.roll(x, 1, axis=0))
        return x + mu_ref[...] * (prev - x)

    r = shifted(r_ref, pr_ref, mur_ref)
    k = shifted(k_ref, pk_ref, muk_ref)
    v = shifted(v_ref, pv_ref, muv_ref)
    z = shifted(z_ref, pz_ref, muz_ref)
    sm = shifted(sm_ref, psm_ref, mus_ref)

    w_log = -jax.nn.softplus(-(w0_ref[...] + _mm(jnp.tanh(sm), w2_ref[...]))) - 0.5
    ld = -jnp.exp(w_log)
    a = jax.nn.sigmoid(a0_ref[...] + _mm(sm, a2_ref[...]))
    kk = k * kk_ref[...]
    kk = kk / jnp.maximum(jnp.sqrt(_head_sum(kk * kk)), 1e-12)
    k2 = k * (1.0 + (a - 1.0) * ka_ref[...])
    kka = kk * a

    rows = r.shape[0]
    n_sub = rows // t
    rr, cc = _iota((rows, rows), 0), _iota((rows, rows), 1)
    chunk_tril = ((cc <= rr) & (rr // t == cc // t)).astype(BF16)
    lw = _mm_x(chunk_tril, ld, pb=3)
    lw_ex = lw - ld
    per_chunk = lambda row_of: jnp.concatenate(
        [jnp.broadcast_to(lw[c * t + row_of:c * t + row_of + 1, :], (t, GROUP_W)) for c in range(n_sub)], axis=0)
    lw_mid = per_chunk(t // 2 - 1)
    lw_end = per_chunk(t - 1)
    e_in = jnp.exp(lw - lw_mid)
    e_out = jnp.exp(lw_mid - lw)
    al_m = -kk * jnp.exp(lw_ex - lw_mid)
    r_m = r * e_in
    be_m = kka * e_out
    k_m = k2 * e_out
    al_s = -kk * jnp.exp(lw_ex)
    r_s = r * jnp.exp(lw)
    e_end = jnp.exp(lw_end - lw)
    b_h = kka * e_end
    k_h = k2 * e_end

    als_ref[...] = al_s.astype(BF16)
    rs_ref[...] = r_s.astype(BF16)
    bh_ref[...] = b_h.astype(BF16)
    bonus_ref[...] = _head_sum(r * k2 * rk_ref[...]) * v
    gate_ref[...] = jax.nn.silu(z)
    for c in range(n_sub):
        wc_ref[c] = jnp.exp(lw_end[c * t:c * t + 8, :])

    strict = _tril(t, strict=True)
    incl = _tril(t)
    lo = _lo_mask((t, LANES))
    units = [(c, h) for c in range(n_sub) for h in range(GROUP_HEADS)]
    idx = range(len(units))
    rsl = [slice(c * t, (c + 1) * t) for c, _ in units]
    csl = [slice((h // 2) * LANES, (h // 2 + 1) * LANES) for _, h in units]
    masks = [lo if h % 2 == 0 else ~lo for _, h in units]
    mm = lambda x, y, dims=NN: _mmp(x, y, dims, passes=PASSES_INV)
    sel = lambda u, x: jnp.where(masks[u], x[rsl[u], csl[u]], 0.0)
    aa = [_mmp(jnp.concatenate([sel(u, al_m), sel(u, r_m)], axis=0),
               jnp.concatenate([be_m[rsl[u], csl[u]], k_m[rsl[u], csl[u]]], axis=0), NT, passes=PASSES_AA)
          for u in idx]
    a_ab = [jnp.where(strict, x[:t, :t], 0.0) for x in aa]
    a_ak = [jnp.where(strict, x[:t, t:], 0.0) for x in aa]
    a_rb = [jnp.where(incl, x[t:, :t], 0.0) for x in aa]
    a_rk = [jnp.where(incl, x[t:, t:], 0.0) for x in aa]
    v_h = [sel(u, v) for u in idx]
    av = [mm(a_ak[u], v_h[u]) for u in idx]
    y_rk = [mm(a_rk[u], v_h[u]) for u in idx]
    s_vk = [mm(v_h[u], sel(u, k_h), TN) for u in idx]
    tinv = _tri_inverse(a_ab)
    u_ind = [mm(tinv[u], av[u]) for u in idx]
    g_mat = [mm(a_rb[u], tinv[u]) for u in idx]
    y_ind = [mm(a_rb[u], u_ind[u]) + y_rk[u] for u in idx]
    s_c = [mm(u_ind[u], sel(u, b_h), TN) + s_vk[u] for u in idx]
    for u, (c, h) in enumerate(units):
        tg_ref[rsl[u], h * 2 * t:h * 2 * t + t] = tinv[u].astype(BF16)
        tg_ref[rsl[u], h * 2 * t + t:(h + 1) * 2 * t] = g_mat[u].astype(BF16)
    for c in range(n_sub):
        for p in range(2):
            u0 = c * GROUP_HEADS + 2 * p
            rs_, sl = slice(c * t, (c + 1) * t), slice(p * LANES, (p + 1) * LANES)
            uind_ref[rs_, sl] = u_ind[u0] + u_ind[u0 + 1]
            yind_ref[rs_, sl] = y_ind[u0] + y_ind[u0 + 1]
            sconst_ref[rs_, sl] = s_c[u0] + s_c[u0 + 1]


def _rwkv_prep(proj, prm, bsz, n_sub=4):
    m = proj.shape[0]
    rows = n_sub * CHUNK
    sub = rows // 8
    col = lambda cb: pl.BlockSpec((rows, GROUP_W), lambda i, cb=cb: (i, cb))
    prev = lambda cb: pl.BlockSpec((8, GROUP_W), lambda i, cb=cb: (jnp.maximum(i * sub - 1, 0), cb))
    full = lambda a: pl.BlockSpec(a.shape, lambda i, n=a.ndim: (0,) * n)
    wide = pl.BlockSpec((rows, GROUP_W), lambda i: (i, 0))
    params = [prm["mu_r"], prm["mu_k"], prm["mu_v"], prm["mu_z"], prm["mu_s"], prm["w0"], prm["w2p"],
              prm["a0"], prm["a2p"], prm["k_k"], prm["k_a"], prm["r_k"]]
    wide_bf = jax.ShapeDtypeStruct((m, GROUP_W), BF16)
    wide_f = jax.ShapeDtypeStruct((m, GROUP_W), F32)
    return pl.pallas_call(
        functools.partial(_rwkv_prep_kernel, steps_per_seq=m // bsz // rows),
        grid=(m // rows,),
        in_specs=[col(CB["d_r"]), col(CB["d_k"]), col(CB["d_v"]), col(CB["d_z"]),
                  pl.BlockSpec((rows, LANES), lambda i: (i, SMALL_CB)),
                  prev(CB["d_r"]), prev(CB["d_k"]), prev(CB["d_v"]), prev(CB["d_z"]),
                  pl.BlockSpec((8, LANES), lambda i: (jnp.maximum(i * sub - 1, 0), SMALL_CB))]
        + [full(a) for a in params],
        out_specs=[wide, wide, wide,
                   pl.BlockSpec((rows, 4 * GROUP_W), lambda i: (i, 0)),
                   wide, wide, wide, wide, wide,
                   pl.BlockSpec((n_sub, 8, GROUP_W), lambda i: (i, 0, 0))],
        out_shape=[wide_bf, wide_bf, wide_bf, jax.ShapeDtypeStruct((m, 4 * GROUP_W), BF16),
                   wide_f, wide_f, wide_f, wide_f, wide_f,
                   jax.ShapeDtypeStruct((m // CHUNK, 8, GROUP_W), F32)],
        compiler_params=_cparams(("parallel",)),
        name="rwkv_prep",
    )(proj, proj, proj, proj, proj, proj, proj, proj, proj, proj, *params)


def _rwkv_scan_kernel(als_ref, rs_ref, bh_ref, tg_ref, yind_ref, bonus_ref, gate_ref, sconst_ref, wc_ref,
                      g_ref, out_ref, s_scr):
    t = CHUNK

    @pl.when(pl.program_id(1) == 0)
    def _():
        s_scr[...] = jnp.zeros_like(s_scr)

    nb = als_ref.shape[0]
    lo = _lo_mask((t, LANES))
    blockdiag = (_iota((LANES, LANES), 0) < HEAD_DIM) == (_iota((LANES, LANES), 1) < HEAD_DIM)
    zero = jnp.zeros((), F32)
    sls = [slice(p * LANES, (p + 1) * LANES) for p in range(2)]
    pairs = [(b, p) for b in range(nb) for p in range(2)]
    units = [(b, h) for b in range(nb) for h in range(GROUP_HEADS)]
    s_ps = [s_scr[b, p] for b, p in pairs]
    x = [_mm_x(jnp.concatenate([als_ref[b, :, sls[p]], rs_ref[b, :, sls[p]]], axis=0), s_ps[i], NT,
               pb=PASSES_STATE) for i, (b, p) in enumerate(pairs)]
    tg = [jnp.concatenate([tg_ref[b, :, h * 2 * t:h * 2 * t + t], tg_ref[b, :, h * 2 * t + t:(h + 1) * 2 * t]],
                          axis=0) for b, h in units]
    res = [_mm_x(tg[u], jnp.where(lo if h % 2 == 0 else ~lo, x[2 * b + h // 2][:t], zero), pb=PASSES_STATE)
           for u, (b, h) in enumerate(units)]
    tx = [res[GROUP_HEADS * b + 2 * p][:t] + res[GROUP_HEADS * b + 2 * p + 1][:t] for b, p in pairs]
    gx = [res[GROUP_HEADS * b + 2 * p][t:] + res[GROUP_HEADS * b + 2 * p + 1][t:] for b, p in pairs]
    upd = [_mm_x(tx[i], bh_ref[b, :, sls[p]], TN, pa=PASSES_STATE) for i, (b, p) in enumerate(pairs)]
    for i, (b, p) in enumerate(pairs):
        s_scr[b, p] = (s_ps[i] * wc_ref[b, 0][0:1, sls[p]] + sconst_ref[b, :, sls[p]]
                       + jnp.where(blockdiag, upd[i], zero))
    for b in range(nb):
        y_tiles = [x[2 * b + p][t:] + gx[2 * b + p] + yind_ref[b, :, sls[p]] for p in range(2)]
        y = _head_rms(jnp.concatenate(y_tiles, axis=1), g_ref[...]) + bonus_ref[b]
        out_ref[b] = (y * gate_ref[b]).astype(out_ref.dtype)


def _rwkv_scan(prep, g, bsz, nb=4):
    als, rs, bh, tg, _uind, yind, bonus, gate, sconst, wc = prep
    m = als.shape[0]
    nb = min(nb, bsz)
    seq = m // bsz
    nc = seq // CHUNK
    per_batch = lambda a: a.reshape((bsz, seq) + a.shape[1:])
    wide = pl.BlockSpec((nb, CHUNK, GROUP_W), lambda b, c: (b, c, 0))
    out = pl.pallas_call(
        _rwkv_scan_kernel,
        grid=(bsz // nb, nc),
        in_specs=[wide, wide, wide,
                  pl.BlockSpec((nb, CHUNK, 4 * GROUP_W), lambda b, c: (b, c, 0)),
                  wide, wide, wide, wide,
                  pl.BlockSpec((nb, 1, 8, GROUP_W), lambda b, c: (b, c, 0, 0)),
                  pl.BlockSpec(g.shape, lambda b, c: (0, 0))],
        out_specs=wide,
        out_shape=jax.ShapeDtypeStruct((bsz, seq, GROUP_W), BF16),
        scratch_shapes=[pltpu.VMEM((nb, 2, LANES, LANES), F32)],
        compiler_params=_cparams(("parallel", "arbitrary")),
        name="rwkv_scan",
    )(per_batch(als), per_batch(rs), per_batch(bh), per_batch(tg), per_batch(yind), per_batch(bonus),
      per_batch(gate), per_batch(sconst), wc.reshape(bsz, nc, 8, GROUP_W), g)
    return out.reshape(m, GROUP_W)


def _row(v, width=None, offset=0):
    v = v.astype(F32).reshape(-1)
    width = v.shape[0] if width is None else width
    return jnp.zeros((1, width), F32).at[0, offset:offset + v.shape[0]].set(v)


def _w_in_segments():
    gw = GROUP_W
    b0 = 3 * gw
    c0 = b0 + 4 * gw + GROUP_HEADS
    d0 = c0 + 5 * gw + 2 * GROUP_HEADS
    m0 = d0 + 4 * gw + DECAY_RANK + A_RANK
    wide = [0, gw, 2 * gw,
            b0, b0 + gw, b0 + 2 * gw, b0 + 3 * gw + GROUP_HEADS,
            c0, c0 + gw, c0 + 2 * gw, c0 + 3 * gw + 2 * GROUP_HEADS, c0 + 4 * gw + 2 * GROUP_HEADS,
            d0, d0 + gw + DECAY_RANK, d0 + 2 * gw + DECAY_RANK, d0 + 3 * gw + DECAY_RANK + A_RANK,
            m0, m0 + gw]
    narrow = [(b0 + 3 * gw, GROUP_HEADS, SM_BF), (c0 + 3 * gw, 2 * GROUP_HEADS, SM_CI),
              (d0 + gw, DECAY_RANK, SM_WLO), (d0 + 3 * gw + DECAY_RANK, A_RANK, SM_ALO)]
    return wide, narrow, m0 + 2 * gw


def _wlayout_kernel(w_ref, o_ref):
    wide, narrow, _ = _w_in_segments()
    rows = w_ref.shape[0]
    for seg, src in enumerate(wide):
        for half in range(GROUP_W // LANES):
            s = src + half * LANES
            base, off = (s // LANES) * LANES, s % LANES
            win = w_ref[:, base:base + 2 * LANES]
            blk = win[:, :LANES] if off == 0 else pltpu.roll(win, 2 * LANES - off, axis=1)[:, :LANES]
            d = seg * GROUP_W + half * LANES
            o_ref[:, d:d + LANES] = blk.astype(o_ref.dtype)
    lane = _iota((rows, LANES), 1)
    small = jnp.zeros((rows, LANES), F32)
    for src, width, dst in narrow:
        base, off = (src // LANES) * LANES, src % LANES
        win = w_ref[:, base:base + LANES]
        if dst != off:
            win = pltpu.roll(win, (dst - off) % LANES, axis=1)
        small = jnp.where((lane >= dst) & (lane < dst + width), win, small)
    o_ref[:, N_WIDE:] = small.astype(o_ref.dtype)


def _layout_w_in(w_in, rows=256):
    d, n_src = w_in.shape
    assert n_src == _w_in_segments()[2]
    w_pad = jnp.pad(w_in, ((0, 0), (0, NP + LANES - n_src)))
    return pl.pallas_call(
        _wlayout_kernel,
        grid=(d // rows,),
        in_specs=[pl.BlockSpec((rows, NP + LANES), lambda i: (i, 0))],
        out_specs=pl.BlockSpec((rows, NP), lambda i: (i, 0)),
        out_shape=jax.ShapeDtypeStruct((d, NP), BF16),
        compiler_params=_cparams(("parallel",)),
        name="wlayout",
    )(w_pad)


def _layout_mu(mu):
    gw = GROUP_W
    o_w, o_k, o_v, o_a, o_z = gw, gw + DECAY_RANK, 2 * gw + DECAY_RANK, 3 * gw + DECAY_RANK, 3 * gw + DECAY_RANK + A_RANK
    mu_s = jnp.zeros((1, LANES), F32)
    mu_s = mu_s.at[0, SM_WLO:SM_WLO + DECAY_RANK].set(mu[o_w:o_w + DECAY_RANK])
    mu_s = mu_s.at[0, SM_ALO:SM_ALO + A_RANK].set(mu[o_a:o_a + A_RANK])
    return dict(mu_r=_row(mu[0:gw]), mu_k=_row(mu[o_k:o_k + gw]), mu_v=_row(mu[o_v:o_v + gw]),
                mu_z=_row(mu[o_z:o_z + gw]), mu_s=mu_s)


def _pad_rows(w, offset):
    return jnp.zeros((LANES, w.shape[1]), F32).at[offset:offset + w.shape[0]].set(w).astype(BF16)


def _layer(x2, mem2, bsz, n_mem, norm_g, w_in, w_out, sgu_norm_g, sgu_w, sgu_b, fox_q_g, fox_k_g, fox_f_b,
           mlstm_conv_w, mlstm_conv_b, mlstm_i_b, mlstm_f_b, mlstm_out_g,
           rwkv_mu, rwkv_w0, rwkv_w2, rwkv_a0, rwkv_a2, rwkv_k_k, rwkv_k_a, rwkv_r_k, rwkv_ln_g,
           mem_norm_g, mem_w_kv, mem_q_g, mem_k_g):
    tile_h = lambda g: _row(jnp.tile(g, GROUP_HEADS))
    proj = _inproj(x2, _row(norm_g), _layout_w_in(w_in))

    ya = _sgu(proj, _row(sgu_norm_g), sgu_w, jnp.repeat(sgu_b.T, HEAD_DIM, axis=1))

    kaug, qaug_t, v_t = _foxprep(proj, tile_h(fox_q_g), tile_h(fox_k_g), _row(fox_f_b, LANES, SM_BF), bsz)
    yb = _fox(proj, kaug, qaug_t, v_t, bsz)

    cw = jnp.zeros((8, 2 * GROUP_W), F32).at[:CONV_K].set(mlstm_conv_w)
    yc = _mlstm(proj, cw, _row(mlstm_conv_b), _row(mlstm_i_b, LANES, SM_CI), _row(mlstm_f_b, LANES, SM_CF),
                _row(mlstm_out_g), bsz)

    prm = _layout_mu(rwkv_mu)
    prm.update(w0=_row(rwkv_w0), w2p=_pad_rows(rwkv_w2, SM_WLO), a0=_row(rwkv_a0), a2p=_pad_rows(rwkv_a2, SM_ALO),
               k_k=_row(rwkv_k_k), k_a=_row(rwkv_k_a), r_k=_row(rwkv_r_k))
    yd = _rwkv_scan(_rwkv_prep(proj, prm, bsz), _row(rwkv_ln_g), bsz)

    mk, mv = _memkv(mem2, _row(mem_norm_g), mem_w_kv.astype(BF16), tile_h(mem_k_g), n_mem)
    ym = _memattn(proj, mk, mv, tile_h(mem_q_g), bsz, n_mem)

    return _outproj([ya, yb, yc, yd, ym], w_out.astype(BF16), x2)


def kernel(x, mem, norm_g, w_in, w_out, sgu_norm_g, sgu_w, sgu_b, fox_q_g, fox_k_g, fox_f_b, mlstm_conv_w, mlstm_conv_b, mlstm_i_b, mlstm_f_b, mlstm_out_g, rwkv_mu, rwkv_w0, rwkv_w2, rwkv_a0, rwkv_a2, rwkv_k_k, rwkv_k_a, rwkv_r_k, rwkv_ln_g, mem_norm_g, mem_w_kv, mem_q_g, mem_k_g):
    bsz, seq, d = x.shape
    n_mem = mem.shape[1]
    stacked = (norm_g, w_in, w_out, sgu_norm_g, sgu_w, sgu_b, fox_q_g, fox_k_g, fox_f_b,
               mlstm_conv_w, mlstm_conv_b, mlstm_i_b, mlstm_f_b, mlstm_out_g,
               rwkv_mu, rwkv_w0, rwkv_w2, rwkv_a0, rwkv_a2, rwkv_k_k, rwkv_k_a, rwkv_r_k, rwkv_ln_g,
               mem_norm_g, mem_w_kv, mem_q_g, mem_k_g)
    x2 = x.reshape(bsz * seq, d)
    mem2 = mem.reshape(bsz * n_mem, d)
    for layer in range(norm_g.shape[0]):
        x2 = _layer(x2, mem2, bsz, n_mem, *[p[layer] for p in stacked])
    return x2.reshape(bsz, seq, d)
```

```python
import functools

import jax
import jax.numpy as jnp
from jax import lax
from jax.experimental import pallas as pl
from jax.experimental.pallas import tpu as pltpu

F32 = jnp.float32
BF16 = jnp.bfloat16

HEAD_DIM = 64
GROUP_HEADS = 4
GROUP_W = GROUP_HEADS * HEAD_DIM
LANES = 128
CHUNK = 128
CONV_K = 4
EPS = 1e-6
SCALE = HEAD_DIM ** -0.5

CB = dict(a_u=0, a_v=1, a_z=2, b_q=3, b_k=4, b_v=5, b_z=6, c_q=7, c_k=8, c_v=9, c_o=10, c_z=11,
          d_r=12, d_k=13, d_v=14, d_z=15, m_q=16, m_z=17)
N_WIDE = 18 * GROUP_W
SMALL_CB = N_WIDE // LANES
NP = N_WIDE + LANES
SM_BF, SM_CI, SM_CF, SM_WLO, SM_ALO = 0, 4, 8, 16, 32
DECAY_RANK = 16
A_RANK = 16

VMEM_LIMIT = 48 * 1024 * 1024

NN = (((1,), (0,)), ((), ()))
NT = (((1,), (1,)), ((), ()))
TN = (((0,), (0,)), ((), ()))


def _mm(a, b, dims=NN):
    return lax.dot_general(a.astype(BF16), b.astype(BF16), dims, preferred_element_type=F32)


def _split(x, n):
    if x.dtype == BF16 or n == 1:
        return [x.astype(BF16)]
    parts, r = [], x
    for _ in range(n):
        p = r.astype(BF16)
        parts.append(p)
        r = r - p.astype(F32)
    return parts


def _mm_x(a, b, dims=NN, pa=1, pb=1):
    pa_list, pb_list = _split(a, pa), _split(b, pb)
    order = max(len(pa_list), len(pb_list))
    acc = None
    for i, ai in enumerate(pa_list):
        for j, bj in enumerate(pb_list):
            if i + j < order:
                t = lax.dot_general(ai, bj, dims, preferred_element_type=F32)
                acc = t if acc is None else acc + t
    return acc


def _iota(shape, axis):
    return lax.broadcasted_iota(jnp.int32, shape, axis)


def _tril(n, strict=False):
    r, c = _iota((n, n), 0), _iota((n, n), 1)
    return (c < r) if strict else (c <= r)


def _head_sum(x):
    w = x.shape[-1]
    bd = (_iota((w, w), 0) // HEAD_DIM == _iota((w, w), 1) // HEAD_DIM).astype(BF16)
    return _mm_x(x, bd, pa=2)


def _head_rms(x, g):
    ms = _head_sum(x * x) * (1.0 / HEAD_DIM)
    return x * lax.rsqrt(ms + EPS) * g


def _lo_mask(shape):
    return _iota(shape, len(shape) - 1) % LANES < HEAD_DIM


def _cparams(sem, vmem_limit=VMEM_LIMIT):
    return pltpu.CompilerParams(dimension_semantics=sem, vmem_limit_bytes=vmem_limit)


def _inproj_kernel(x_ref, g_ref, w_ref, o_ref, *, col_chunk):
    x = x_ref[...]
    ms = jnp.mean(x * x, axis=-1, keepdims=True)
    h = (x * lax.rsqrt(ms + EPS) * g_ref[...]).astype(BF16)
    n = o_ref.shape[1]
    for c0 in range(0, n, col_chunk):
        c1 = min(c0 + col_chunk, n)
        o_ref[:, c0:c1] = jnp.dot(h, w_ref[:, c0:c1], preferred_element_type=F32)


def _inproj(x2, g, w_bf16, tm=512):
    m, d = x2.shape
    n = w_bf16.shape[1]
    return pl.pallas_call(
        functools.partial(_inproj_kernel, col_chunk=512),
        grid=(m // tm,),
        in_specs=[pl.BlockSpec((tm, d), lambda i: (i, 0)),
                  pl.BlockSpec((1, d), lambda i: (0, 0)),
                  pl.BlockSpec((d, n), lambda i: (0, 0))],
        out_specs=pl.BlockSpec((tm, n), lambda i: (i, 0)),
        out_shape=jax.ShapeDtypeStruct((m, n), F32),
        compiler_params=_cparams(("parallel",), 2 * (2 * d * n + 4 * tm * n + 4 * tm * d) + (8 << 20)),
        name="inproj",
    )(x2, g, w_bf16)


def _outproj_kernel(ya, yb, yc, yd, ym, w_ref, x_ref, o_ref):
    mixed = jnp.concatenate([ya[...], yb[...], yc[...], yd[...], ym[...]], axis=1)
    o_ref[...] = x_ref[...] + jnp.dot(mixed, w_ref[...], preferred_element_type=F32)


def _outproj(ys, w_bf16, x2, tm=1024):
    m, d = x2.shape
    yspec = pl.BlockSpec((tm, GROUP_W), lambda i: (i, 0))
    return pl.pallas_call(
        _outproj_kernel,
        grid=(m // tm,),
        in_specs=[yspec] * 5 + [pl.BlockSpec(w_bf16.shape, lambda i: (0, 0)),
                                pl.BlockSpec((tm, d), lambda i: (i, 0))],
        out_specs=pl.BlockSpec((tm, d), lambda i: (i, 0)),
        out_shape=jax.ShapeDtypeStruct((m, d), F32),
        compiler_params=_cparams(("parallel",)),
        name="outproj",
    )(*ys, w_bf16, x2)


def _sgu_kernel(u_ref, v_ref, z_ref, g_ref, w_ref, bias_ref, o_ref):
    t = u_ref.shape[0]
    u = jax.nn.gelu(u_ref[...])
    vn = _head_rms(jax.nn.gelu(v_ref[...]), g_ref[...])
    gate = jax.nn.silu(z_ref[...])
    tril = _tril(CHUNK)
    lo = _lo_mask((CHUNK, LANES))
    ws = [jnp.where(tril, w_ref[g], 0.0).astype(BF16) for g in range(GROUP_HEADS)]
    bias = bias_ref[...]
    for c in range(t // CHUNK):
        sl = slice(c * CHUNK, (c + 1) * CHUNK)
        tiles = []
        for p in range(2):
            vt = vn[sl, p * LANES:(p + 1) * LANES]
            tiles.append(_mm(ws[2 * p], jnp.where(lo, vt, 0.0)) + _mm(ws[2 * p + 1], jnp.where(lo, 0.0, vt)))
        mixed = jnp.concatenate(tiles, axis=1) + bias
        o_ref[sl, :] = (u[sl] * mixed * gate[sl]).astype(o_ref.dtype)


def _sgu(proj, g, w, bias_tile, t=512):
    m = proj.shape[0]
    col = lambda cb: pl.BlockSpec((t, GROUP_W), lambda i, cb=cb: (i, cb))
    full = lambda a: pl.BlockSpec(a.shape, lambda i, n=a.ndim: (0,) * n)
    return pl.pallas_call(
        _sgu_kernel,
        grid=(m // t,),
        in_specs=[col(CB["a_u"]), col(CB["a_v"]), col(CB["a_z"]), full(g), full(w), full(bias_tile)],
        out_specs=pl.BlockSpec((t, GROUP_W), lambda i: (i, 0)),
        out_shape=jax.ShapeDtypeStruct((m, GROUP_W), BF16),
        compiler_params=_cparams(("parallel",)),
        name="sgu",
    )(proj, proj, proj, g, w, bias_tile)


def _memkv_kernel(mem_ref, g_ref, w_ref, kg_ref, k_ref, v_ref):
    x = mem_ref[...]
    ms = jnp.mean(x * x, axis=-1, keepdims=True)
    h = (x * lax.rsqrt(ms + EPS) * g_ref[...]).astype(BF16)
    kv = jnp.dot(h, w_ref[...], preferred_element_type=F32)
    k_ref[...] = _head_rms(kv[:, :GROUP_W], kg_ref[...]).astype(BF16)
    v_ref[...] = kv[:, GROUP_W:].astype(BF16)


def _memkv(mem2, g, w_bf16, kg, n_mem):
    m, d = mem2.shape
    full = lambda a: pl.BlockSpec(a.shape, lambda i, n=a.ndim: (0,) * n)
    out = pl.BlockSpec((n_mem, GROUP_W), lambda i: (i, 0))
    return pl.pallas_call(
        _memkv_kernel,
        grid=(m // n_mem,),
        in_specs=[pl.BlockSpec((n_mem, d), lambda i: (i, 0)), full(g), full(w_bf16), full(kg)],
        out_specs=[out, out],
        out_shape=[jax.ShapeDtypeStruct((m, GROUP_W), BF16)] * 2,
        compiler_params=_cparams(("parallel",)),
        name="memkv",
    )(mem2, g, w_bf16, kg)


def _memattn_kernel(q_ref, z_ref, k_ref, v_ref, qg_ref, o_ref):
    q = (_head_rms(q_ref[...], qg_ref[...]) * SCALE).astype(BF16)
    k = k_ref[...]
    v = v_ref[...]
    t = q.shape[0]
    lo_q = _lo_mask((t, LANES))
    lo_v = _lo_mask((v.shape[0], LANES))
    zero = jnp.zeros((), BF16)
    heads = range(GROUP_HEADS)
    sls = [slice((h // 2) * LANES, (h // 2 + 1) * LANES) for h in heads]
    s = [_mm(jnp.where(lo_q if h % 2 == 0 else ~lo_q, q[:, sls[h]], zero), k[:, sls[h]], NT) for h in heads]
    e = [jnp.exp(s[h] - jnp.max(s[h], axis=-1, keepdims=True)) for h in heads]
    pr = [e[h] / jnp.sum(e[h], axis=-1, keepdims=True) for h in heads]
    pv = [_mm(pr[h], jnp.where(lo_v if h % 2 == 0 else ~lo_v, v[:, sls[h]], zero)) for h in heads]
    y = jnp.concatenate([pv[0] + pv[1], pv[2] + pv[3]], axis=1)
    o_ref[...] = (y * jax.nn.silu(z_ref[...])).astype(o_ref.dtype)


def _memattn(proj, kn, vb, qg, bsz, n_mem, t=512):
    m = proj.shape[0]
    nt = m // bsz // t
    col = lambda cb: pl.BlockSpec((t, GROUP_W), lambda b, i, cb=cb: (b * nt + i, cb))
    kv = pl.BlockSpec((n_mem, GROUP_W), lambda b, i: (b, 0))
    return pl.pallas_call(
        _memattn_kernel,
        grid=(bsz, nt),
        in_specs=[col(CB["m_q"]), col(CB["m_z"]), kv, kv, pl.BlockSpec(qg.shape, lambda b, i: (0, 0))],
        out_specs=pl.BlockSpec((t, GROUP_W), lambda b, i: (b * nt + i, 0)),
        out_shape=jax.ShapeDtypeStruct((m, GROUP_W), BF16),
        compiler_params=_cparams(("parallel", "parallel")),
        name="memattn",
    )(proj, proj, kn, vb, qg)


LOG2E = 1.4426950408889634
F_TERMS = 3
V_ROWS = 80


def _bf16_terms(x, n):
    terms, r = [], x
    for _ in range(n):
        p = r.astype(BF16).astype(F32)
        terms.append(p)
        r = r - p
    return terms


def _foxprep_kernel(q_ref, k_ref, v_ref, sm_ref, qg_ref, kg_ref, fb_ref,
                    kaug_ref, qaug_t_ref, v_t_ref, carry_ref):
    @pl.when(pl.program_id(1) == 0)
    def _():
        carry_ref[...] = jnp.zeros_like(carry_ref)

    t = q_ref.shape[0]
    qn = _head_rms(q_ref[...], qg_ref[...]) * (SCALE * LOG2E)
    kn = _head_rms(k_ref[...], kg_ref[...])
    lf = jax.nn.log_sigmoid(sm_ref[...] + fb_ref[...])
    cs = _mm_x(_tril(t).astype(BF16), lf, pb=3) + carry_ref[0:1, :]
    carry_ref[...] = jnp.broadcast_to(cs[t - 1:t, :], carry_ref.shape)
    f2 = cs * LOG2E

    lane = _iota((t, LANES), 1)
    q_tiles = []
    for h in range(GROUP_HEADS):
        sl = slice((h // 2) * LANES, (h // 2 + 1) * LANES)
        q_t, k_t = qn[:, sl], kn[:, sl]
        if h % 2 == 1:
            q_t = pltpu.roll(q_t, HEAD_DIM, axis=1)
            k_t = pltpu.roll(k_t, HEAD_DIM, axis=1)
        terms = _bf16_terms(f2[:, SM_BF + h:SM_BF + h + 1], F_TERMS)
        qa = jnp.where(lane < HEAD_DIM + 2 * F_TERMS, 1.0, 0.0)
        ka = qa
        for n, term in enumerate(terms):
            qa = jnp.where(lane == HEAD_DIM + n, term, qa)
            ka = jnp.where(lane == HEAD_DIM + F_TERMS + n, -term, ka)
        qa = jnp.where(lane < HEAD_DIM, q_t, qa)
        ka = jnp.where(lane < HEAD_DIM, k_t, ka)
        kaug_ref[:, h * LANES:(h + 1) * LANES] = ka.astype(BF16)
        q_tiles.append(qa)
    qaug_t_ref[0] = jnp.concatenate(q_tiles, axis=1).T.astype(BF16)
    v_t = v_ref[...].T
    pad_rows = jnp.where(_iota((V_ROWS - HEAD_DIM, t), 0) == 0, 1.0, 0.0)
    v_t_ref[0] = jnp.concatenate(
        [x for h in range(GROUP_HEADS) for x in (v_t[h * HEAD_DIM:(h + 1) * HEAD_DIM], pad_rows)],
        axis=0).astype(BF16)


def _foxprep(proj, qg, kg, fb_row, bsz, t=512):
    m = proj.shape[0]
    seq = m // bsz
    nt = seq // t
    col = lambda cb: pl.BlockSpec((t, GROUP_W), lambda b, i, cb=cb: (b * nt + i, cb))
    row = lambda a: pl.BlockSpec(a.shape, lambda b, i: (0, 0))
    aug_w = GROUP_HEADS * LANES
    return pl.pallas_call(
        _foxprep_kernel,
        grid=(bsz, nt),
        in_specs=[col(CB["b_q"]), col(CB["b_k"]), col(CB["b_v"]),
                  pl.BlockSpec((t, LANES), lambda b, i: (b * nt + i, SMALL_CB)),
                  row(qg), row(kg), row(fb_row)],
        out_specs=[pl.BlockSpec((t, aug_w), lambda b, i: (b * nt + i, 0)),
                   pl.BlockSpec((1, aug_w, t), lambda b, i: (b, 0, i)),
                   pl.BlockSpec((1, GROUP_HEADS * V_ROWS, t), lambda b, i: (b, 0, i))],
        out_shape=[jax.ShapeDtypeStruct((m, aug_w), BF16),
                   jax.ShapeDtypeStruct((bsz, aug_w, seq), BF16),
                   jax.ShapeDtypeStruct((bsz, GROUP_HEADS * V_ROWS, seq), BF16)],
        scratch_shapes=[pltpu.VMEM((8, LANES), F32)],
        compiler_params=_cparams(("parallel", "arbitrary")),
        name="foxprep",
    )(proj, proj, proj, proj, qg, kg, fb_row)


def _fox_kernel(q_ref, k_ref, v_ref, z_ref, o_ref, m_scr, acc_scr, sa_scr, sb_scr, mxa_scr, mxb_scr,
                *, tq, tk):
    i = pl.program_id(2)
    m_scr[...] = jnp.full_like(m_scr, -jnp.inf)
    acc_scr[...] = jnp.zeros_like(acc_scr)

    def scores(j, s_scr, mx_scr, q0=0):
        k0 = pl.multiple_of(j * tk, tk)
        for hh in range(2):
            k_blk = k_ref[pl.ds(k0, tk), hh * LANES:(hh + 1) * LANES]
            s_t = jnp.dot(k_blk, q_ref[0, hh * LANES:(hh + 1) * LANES, q0:],
                          preferred_element_type=F32)
            s_scr[hh, :, q0:] = s_t
            mx_scr[hh, :, q0:] = jnp.broadcast_to(jnp.max(s_t, axis=0, keepdims=True), (8, tq - q0))

    def softmax_pv(j, s_scr, mx_scr, masked=False, q0=0):
        k0 = pl.multiple_of(j * tk, tk)
        nq_ = tq - q0
        for hh in range(2):
            s_t = s_scr[hh, :, q0:]
            mx = mx_scr[hh, 0:1, q0:]
            if masked:
                visible = (k0 + _iota((tk, nq_), 0)) <= (i * tq + q0 + _iota((tk, nq_), 1))
                s_t = jnp.where(visible, s_t, -jnp.inf)
                mx = jnp.max(s_t, axis=0, keepdims=True)
            m_old = m_scr[hh, 0:1, q0:]
            m_new = jnp.maximum(m_old, mx)
            alpha = jnp.exp2(m_old - m_new)
            p_t = jnp.exp2(s_t - m_new)
            m_scr[hh, :, q0:] = jnp.broadcast_to(m_new, (8, nq_))
            v_blk = v_ref[0, hh * V_ROWS:(hh + 1) * V_ROWS, pl.ds(k0, tk)]
            acc_scr[hh, :, q0:] = (acc_scr[hh, :, q0:] * alpha
                                   + jnp.dot(v_blk, p_t.astype(BF16), preferred_element_type=F32))

    assert tq == 2 * tk
    n_full = 2 * i
    scores(0, sa_scr, mxa_scr)

    def two_blocks(jj, carry):
        scores(2 * jj + 1, sb_scr, mxb_scr)
        softmax_pv(2 * jj, sa_scr, mxa_scr)
        scores(2 * jj + 2, sa_scr, mxa_scr)
        softmax_pv(2 * jj + 1, sb_scr, mxb_scr)
        return carry

    lax.fori_loop(0, i, two_blocks, 0)
    scores(n_full + 1, sb_scr, mxb_scr, q0=tk)
    softmax_pv(n_full, sa_scr, mxa_scr, masked=True)
    softmax_pv(n_full + 1, sb_scr, mxb_scr, masked=True, q0=tk)

    y_t = jnp.concatenate([acc_scr[hh][:HEAD_DIM] / acc_scr[hh][HEAD_DIM:HEAD_DIM + 1] for hh in range(2)],
                          axis=0)
    o_ref[...] = (y_t.T * jax.nn.silu(z_ref[...])).astype(o_ref.dtype)


def _fox(proj, kaug, qaug_t, v_t, bsz, tq=512, tk=256):
    m = proj.shape[0]
    seq = m // bsz
    nq = seq // tq
    s_buf = pltpu.VMEM((2, tk, tq), F32)
    mx_buf = pltpu.VMEM((2, 8, tq), F32)
    return pl.pallas_call(
        functools.partial(_fox_kernel, tq=tq, tk=tk),
        grid=(bsz, 2, nq),
        in_specs=[pl.BlockSpec((1, 2 * LANES, tq), lambda b, p, i: (b, p, i)),
                  pl.BlockSpec((seq, 2 * LANES), lambda b, p, i: (b, p)),
                  pl.BlockSpec((1, 2 * V_ROWS, seq), lambda b, p, i: (b, p, 0)),
                  pl.BlockSpec((tq, LANES), lambda b, p, i: (b * nq + i, 2 * CB["b_z"] + p))],
        out_specs=pl.BlockSpec((tq, LANES), lambda b, p, i: (b * nq + i, p)),
        out_shape=jax.ShapeDtypeStruct((m, GROUP_W), BF16),
        scratch_shapes=[pltpu.VMEM((2, 8, tq), F32), pltpu.VMEM((2, V_ROWS, tq), F32),
                        s_buf, s_buf, mx_buf, mx_buf],
        compiler_params=_cparams(("parallel", "parallel", "arbitrary")),
        name="fox",
    )(qaug_t, kaug, v_t, proj)


def _mlstm_kernel(q_ref, k_ref, v_ref, og_ref, z_ref, sm_ref, cw_ref, cb_ref, ib_ref, fb_ref, g_ref,
                  out_ref, xbuf, c_scr, n_scr, m_scr):
    t = CHUNK
    nb = q_ref.shape[0]

    @pl.when(pl.program_id(1) == 0)
    def _():
        xbuf[:, 0:8, :] = jnp.zeros((nb, 8, 2 * GROUP_W), F32)
        c_scr[...] = jnp.zeros_like(c_scr)
        n_scr[...] = jnp.zeros_like(n_scr)
        m_scr[...] = jnp.zeros_like(m_scr)

    tril = _tril(t)
    tril_bf = tril.astype(BF16)
    lo = _lo_mask((t, LANES))
    lo_row = _lo_mask((1, LANES))
    blockdiag = (_iota((LANES, LANES), 0) < HEAD_DIM) == (_iota((LANES, LANES), 1) < HEAD_DIM)
    lane_row = _iota((1, LANES), 1)
    sls = [slice(p * LANES, (p + 1) * LANES) for p in range(2)]

    qs, ks, vs, lis, bcums, b_ts, i_ts, m_rows = [], [], [], [], [], [], [], []
    for b in range(nb):
        xbuf[b, 8:8 + t, 0:GROUP_W] = q_ref[b]
        xbuf[b, 8:8 + t, GROUP_W:] = k_ref[b]
        conv = cb_ref[...]
        for jj in range(CONV_K):
            conv = conv + cw_ref[jj:jj + 1, :] * xbuf[b, 8 - (CONV_K - 1) + jj:8 - (CONV_K - 1) + jj + t, :]
        xbuf[b, 0:8, :] = xbuf[b, t:t + 8, :]
        qk_act = jax.nn.silu(conv)
        qs.append(qk_act[:, :GROUP_W])
        ks.append(qk_act[:, GROUP_W:] * SCALE)
        vs.append(v_ref[b])
        sm = sm_ref[b]
        li = sm + ib_ref[...]
        lf = jax.nn.log_sigmoid(sm + fb_ref[...])
        bcum = _mm_x(tril_bf, lf, pb=3)
        lis.append(li)
        bcums.append(bcum)
        b_ts.append(bcum.T)
        i_ts.append(li.T)
        m_rows.append(m_scr[b, 0:1, :])

    units = [(b, h) for b in range(nb) for h in range(GROUP_HEADS)]
    pairs = [(b, p) for b in range(nb) for p in range(2)]
    idx = range(len(units))
    pair_of = [2 * b + h // 2 for b, h in units]
    masks = [lo if h % 2 == 0 else ~lo for _, h in units]
    qp = [qs[b][:, sls[p]] for b, p in pairs]
    kp = [ks[b][:, sls[p]] for b, p in pairs]
    vp = [vs[b][:, sls[p]] for b, p in pairs]
    c_ps = [c_scr[b, p] for b, p in pairs]
    n_ps = [n_scr[b, p][0:1, :] for b, p in pairs]
    qk = [_mm(jnp.where(masks[u], qp[pair_of[u]], 0.0), kp[pair_of[u]], NT) for u in idx]
    q_c = [_mm(qp[i], c_ps[i]) for i in range(len(pairs))]
    q_n = [qp[i] * n_ps[i] for i in range(len(pairs))]
    b_c = [bcums[b][:, SM_CF + h:SM_CF + h + 1] for b, h in units]
    b_r = [b_ts[b][SM_CF + h:SM_CF + h + 1, :] for b, h in units]
    i_c = [lis[b][:, SM_CI + h:SM_CI + h + 1] for b, h in units]
    i_r = [i_ts[b][SM_CI + h:SM_CI + h + 1, :] for b, h in units]
    m_st = [m_rows[b][:, h:h + 1] for b, h in units]
    log_d = [jnp.where(tril, b_c[u] - b_r[u] + i_r[u], -jnp.inf) for u in idx]
    inter = [b_c[u] + m_st[u] for u in idx]
    m_t = [jnp.maximum(jnp.max(log_d[u], axis=-1, keepdims=True), inter[u]) for u in idx]
    s = [qk[u] * jnp.exp(log_d[u] - m_t[u]) for u in idx]
    w_inter = [jnp.exp(inter[u] - m_t[u]) for u in idx]
    sv = [_mm(s[u], jnp.where(masks[u], vp[pair_of[u]], 0.0)) for u in idx]
    num = [sv[u] + w_inter[u] * jnp.where(masks[u], q_c[pair_of[u]], 0.0) for u in idx]
    den = [jnp.sum(s[u], axis=-1, keepdims=True)
           + w_inter[u] * jnp.sum(jnp.where(masks[u], q_n[pair_of[u]], 0.0), axis=-1, keepdims=True) for u in idx]
    hv = [num[u] / jnp.maximum(jnp.abs(den[u]), jnp.exp(-m_t[u])) for u in idx]
    g = [b_c[u][t - 1:t, :] for u in idx]
    m_new = [jnp.maximum(g[u] + m_st[u], jnp.max(g[u] - b_r[u] + i_r[u], axis=-1, keepdims=True)) for u in idx]
    w_col = [jnp.exp(g[u] - b_c[u] + i_c[u] - m_new[u]) for u in idx]
    cd = [jnp.exp(g[u] + m_st[u] - m_new[u]) for u in idx]
    first = lambda b, p: GROUP_HEADS * b + 2 * p
    kw = [kp[i] * jnp.where(lo, w_col[first(b, p)], w_col[first(b, p) + 1]) for i, (b, p) in enumerate(pairs)]
    cd_row = [jnp.where(lo_row, cd[first(b, p)], cd[first(b, p) + 1]) for b, p in pairs]
    upd = [_mm(kw[i], vp[i], TN) for i in range(len(pairs))]
    for i, (b, p) in enumerate(pairs):
        c_scr[b, p] = cd_row[i] * c_ps[i] + jnp.where(blockdiag, upd[i], 0.0)
        n_new = cd_row[i] * n_ps[i] + jnp.sum(kw[i], axis=0, keepdims=True)
        n_scr[b, p] = jnp.broadcast_to(n_new, (8, LANES))
    for b in range(nb):
        m_row_new = m_rows[b]
        for h in range(GROUP_HEADS):
            m_row_new = jnp.where(lane_row == h, m_new[GROUP_HEADS * b + h], m_row_new)
        m_scr[b] = jnp.broadcast_to(m_row_new, (8, LANES))
        h_tiles = [hv[first(b, p)] + hv[first(b, p) + 1] for p in range(2)]
        hcat = jax.nn.sigmoid(og_ref[b]) * jnp.concatenate(h_tiles, axis=1)
        y = _head_rms(hcat, g_ref[...])
        out_ref[b] = (y * jax.nn.silu(z_ref[b])).astype(out_ref.dtype)


def _mlstm(proj, cw, cb, ib_row, fb_row, g, bsz, nb=4):
    m = proj.shape[0]
    seq = m // bsz
    nc = seq // CHUNK
    nb = min(nb, bsz)
    proj3 = proj.reshape(bsz, seq, proj.shape[1])
    col = lambda cb_: pl.BlockSpec((nb, CHUNK, GROUP_W), lambda b, c, cb_=cb_: (b, c, cb_))
    full = lambda a: pl.BlockSpec(a.shape, lambda b, c, n=a.ndim: (0,) * n)
    out = pl.pallas_call(
        _mlstm_kernel,
        grid=(bsz // nb, nc),
        in_specs=[col(CB["c_q"]), col(CB["c_k"]), col(CB["c_v"]), col(CB["c_o"]), col(CB["c_z"]),
                  pl.BlockSpec((nb, CHUNK, LANES), lambda b, c: (b, c, SMALL_CB)),
                  full(cw), full(cb), full(ib_row), full(fb_row), full(g)],
        out_specs=pl.BlockSpec((nb, CHUNK, GROUP_W), lambda b, c: (b, c, 0)),
        out_shape=jax.ShapeDtypeStruct((bsz, seq, GROUP_W), BF16),
        scratch_shapes=[pltpu.VMEM((nb, CHUNK + 8, 2 * GROUP_W), F32), pltpu.VMEM((nb, 2, LANES, LANES), F32),
                        pltpu.VMEM((nb, 2, 8, LANES), F32), pltpu.VMEM((nb, 8, LANES), F32)],
        compiler_params=_cparams(("parallel", "arbitrary")),
        name="mlstm",
    )(proj3, proj3, proj3, proj3, proj3, proj3, cw, cb, ib_row, fb_row, g)
    return out.reshape(m, GROUP_W)


PASSES_AA = 1
PASSES_INV = 1
PASSES_STATE = 1


def _mmp(a, b, dims=NN, passes=1):
    return _mm_x(a, b, dims, pa=passes, pb=passes)


def _tri_inverse(mats):
    n = mats[0].shape[0]
    r, c = _iota((n, n), 0), _iota((n, n), 1)
    same = lambda b: (r // b) == (c // b)
    mm = lambda x, y: _mmp(x, y, passes=PASSES_INV)
    eye = jnp.where(r == c, 1.0, 0.0)
    a8 = [jnp.where(same(8), a, 0.0) for a in mats]
    ts = [eye + x for x in a8]
    ps = [mm(x, x) for x in a8]
    ts = [t + mm(t, p) for t, p in zip(ts, ps)]
    ps = [mm(p, p) for p in ps]
    ts = [t + mm(t, p) for t, p in zip(ts, ps)]
    b = 8
    while b < n:
        off = same(2 * b) & ~same(b)
        inner = [mm(jnp.where(off, a, 0.0), t) for a, t in zip(mats, ts)]
        ts = [t + mm(t, x) for t, x in zip(ts, inner)]
        b *= 2
    return ts


def _rwkv_prep_kernel(r_ref, k_ref, v_ref, z_ref, sm_ref, pr_ref, pk_ref, pv_ref, pz_ref, psm_ref,
                      mur_ref, muk_ref, muv_ref, muz_ref, mus_ref,
                      w0_ref, w2_ref, a0_ref, a2_ref, kk_ref, ka_ref, rk_ref,
                      als_ref, rs_ref, bh_ref, tg_ref, uind_ref, yind_ref, bonus_ref, gate_ref,
                      sconst_ref, wc_ref, *, steps_per_seq):
    t = CHUNK
    first = (pl.program_id(0) % steps_per_seq) == 0
    row0 = _iota((r_ref.shape[0], 1), 0) == 0

    def shifted(x_ref, p_ref, mu_ref):
        x = x_ref[...]
        prev_row = jnp.where(first, 0.0, p_ref[7:8, :])
        prev = jnp.where(row0, prev_row, pltpu.roll(x, 1, axis=0))
        return x + mu_ref[...] * (prev - x)

    r = shifted(r_ref, pr_ref, mur_ref)
    k = shifted(k_ref, pk_ref, muk_ref)
    v = shifted(v_ref, pv_ref, muv_ref)
    z = shifted(z_ref, pz_ref, muz_ref)
    sm = shifted(sm_ref, psm_ref, mus_ref)

    w_log = -jax.nn.softplus(-(w0_ref[...] + _mm(jnp.tanh(sm), w2_ref[...]))) - 0.5
    ld = -jnp.exp(w_log)
    a = jax.nn.sigmoid(a0_ref[...] + _mm(sm, a2_ref[...]))
    kk = k * kk_ref[...]
    kk = kk / jnp.maximum(jnp.sqrt(_head_sum(kk * kk)), 1e-12)
    k2 = k * (1.0 + (a - 1.0) * ka_ref[...])
    kka = kk * a

    rows = r.shape[0]
    n_sub = rows // t
    rr, cc = _iota((rows, rows), 0), _iota((rows, rows), 1)
    chunk_tril = ((cc <= rr) & (rr // t == cc // t)).astype(BF16)
    lw = _mm_x(chunk_tril, ld, pb=3)
    lw_ex = lw - ld
    per_chunk = lambda row_of: jnp.concatenate(
        [jnp.broadcast_to(lw[c * t + row_of:c * t + row_of + 1, :], (t, GROUP_W)) for c in range(n_sub)], axis=0)
    lw_mid = per_chunk(t // 2 - 1)
    lw_end = per_chunk(t - 1)
    e_in = jnp.exp(lw - lw_mid)
    e_out = jnp.exp(lw_mid - lw)
    al_m = -kk * jnp.exp(lw_ex - lw_mid)
    r_m = r * e_in
    be_m = kka * e_out
    k_m = k2 * e_out
    al_s = -kk * jnp.exp(lw_ex)
    r_s = r * jnp.exp(lw)
    e_end = jnp.exp(lw_end - lw)
    b_h = kka * e_end
    k_h = k2 * e_end

    als_ref[...] = al_s.astype(BF16)
    rs_ref[...] = r_s.astype(BF16)
    bh_ref[...] = b_h.astype(BF16)
    bonus_ref[...] = _head_sum(r * k2 * rk_ref[...]) * v
    gate_ref[...] = jax.nn.silu(z)
    for c in range(n_sub):
        wc_ref[c] = jnp.exp(lw_end[c * t:c * t + 8, :])

    strict = _tril(t, strict=True)
    incl = _tril(t)
    lo = _lo_mask((t, LANES))
    units = [(c, h) for c in range(n_sub) for h in range(GROUP_HEADS)]
    idx = range(len(units))
    rsl = [slice(c * t, (c + 1) * t) for c, _ in units]
    csl = [slice((h // 2) * LANES, (h // 2 + 1) * LANES) for _, h in units]
    masks = [lo if h % 2 == 0 else ~lo for _, h in units]
    mm = lambda x, y, dims=NN: _mmp(x, y, dims, passes=PASSES_INV)
    sel = lambda u, x: jnp.where(masks[u], x[rsl[u], csl[u]], 0.0)
    aa = [_mmp(jnp.concatenate([sel(u, al_m), sel(u, r_m)], axis=0),
               jnp.concatenate([be_m[rsl[u], csl[u]], k_m[rsl[u], csl[u]]], axis=0), NT, passes=PASSES_AA)
          for u in idx]
    a_ab = [jnp.where(strict, x[:t, :t], 0.0) for x in aa]
    a_ak = [jnp.where(strict, x[:t, t:], 0.0) for x in aa]
    a_rb = [jnp.where(incl, x[t:, :t], 0.0) for x in aa]
    a_rk = [jnp.where(incl, x[t:, t:], 0.0) for x in aa]
    v_h = [sel(u, v) for u in idx]
    av = [mm(a_ak[u], v_h[u]) for u in idx]
    y_rk = [mm(a_rk[u], v_h[u]) for u in idx]
    s_vk = [mm(v_h[u], sel(u, k_h), TN) for u in idx]
    tinv = _tri_inverse(a_ab)
    u_ind = [mm(tinv[u], av[u]) for u in idx]
    g_mat = [mm(a_rb[u], tinv[u]) for u in idx]
    y_ind = [mm(a_rb[u], u_ind[u]) + y_rk[u] for u in idx]
    s_c = [mm(u_ind[u], sel(u, b_h), TN) + s_vk[u] for u in idx]
    for u, (c, h) in enumerate(units):
        tg_ref[rsl[u], h * 2 * t:h * 2 * t + t] = tinv[u].astype(BF16)
        tg_ref[rsl[u], h * 2 * t + t:(h + 1) * 2 * t] = g_mat[u].astype(BF16)
    for c in range(n_sub):
        for p in range(2):
            u0 = c * GROUP_HEADS + 2 * p
            rs_, sl = slice(c * t, (c + 1) * t), slice(p * LANES, (p + 1) * LANES)
            uind_ref[rs_, sl] = u_ind[u0] + u_ind[u0 + 1]
            yind_ref[rs_, sl] = y_ind[u0] + y_ind[u0 + 1]
            sconst_ref[rs_, sl] = s_c[u0] + s_c[u0 + 1]


def _rwkv_prep(proj, prm, bsz, n_sub=4):
    m = proj.shape[0]
    rows = n_sub * CHUNK
    sub = rows // 8
    col = lambda cb: pl.BlockSpec((rows, GROUP_W), lambda i, cb=cb: (i, cb))
    prev = lambda cb: pl.BlockSpec((8, GROUP_W), lambda i, cb=cb: (jnp.maximum(i * sub - 1, 0), cb))
    full = lambda a: pl.BlockSpec(a.shape, lambda i, n=a.ndim: (0,) * n)
    wide = pl.BlockSpec((rows, GROUP_W), lambda i: (i, 0))
    params = [prm["mu_r"], prm["mu_k"], prm["mu_v"], prm["mu_z"], prm["mu_s"], prm["w0"], prm["w2p"],
              prm["a0"], prm["a2p"], prm["k_k"], prm["k_a"], prm["r_k"]]
    wide_bf = jax.ShapeDtypeStruct((m, GROUP_W), BF16)
    wide_f = jax.ShapeDtypeStruct((m, GROUP_W), F32)
    return pl.pallas_call(
        functools.partial(_rwkv_prep_kernel, steps_per_seq=m // bsz // rows),
        grid=(m // rows,),
        in_specs=[col(CB["d_r"]), col(CB["d_k"]), col(CB["d_v"]), col(CB["d_z"]),
                  pl.BlockSpec((rows, LANES), lambda i: (i, SMALL_CB)),
                  prev(CB["d_r"]), prev(CB["d_k"]), prev(CB["d_v"]), prev(CB["d_z"]),
                  pl.BlockSpec((8, LANES), lambda i: (jnp.maximum(i * sub - 1, 0), SMALL_CB))]
        + [full(a) for a in params],
        out_specs=[wide, wide, wide,
                   pl.BlockSpec((rows, 4 * GROUP_W), lambda i: (i, 0)),
                   wide, wide, wide, wide, wide,
                   pl.BlockSpec((n_sub, 8, GROUP_W), lambda i: (i, 0, 0))],
        out_shape=[wide_bf, wide_bf, wide_bf, jax.ShapeDtypeStruct((m, 4 * GROUP_W), BF16),
                   wide_f, wide_f, wide_f, wide_f, wide_f,
                   jax.ShapeDtypeStruct((m // CHUNK, 8, GROUP_W), F32)],
        compiler_params=_cparams(("parallel",)),
        name="rwkv_prep",
    )(proj, proj, proj, proj, proj, proj, proj, proj, proj, proj, *params)


def _rwkv_scan_kernel(als_ref, rs_ref, bh_ref, tg_ref, yind_ref, bonus_ref, gate_ref, sconst_ref, wc_ref,
                      g_ref, out_ref, s_scr):
    t = CHUNK

    @pl.when(pl.program_id(1) == 0)
    def _():
        s_scr[...] = jnp.zeros_like(s_scr)

    nb = als_ref.shape[0]
    lo = _lo_mask((t, LANES))
    blockdiag = (_iota((LANES, LANES), 0) < HEAD_DIM) == (_iota((LANES, LANES), 1) < HEAD_DIM)
    zero = jnp.zeros((), F32)
    sls = [slice(p * LANES, (p + 1) * LANES) for p in range(2)]
    pairs = [(b, p) for b in range(nb) for p in range(2)]
    units = [(b, h) for b in range(nb) for h in range(GROUP_HEADS)]
    s_ps = [s_scr[b, p] for b, p in pairs]
    x = [_mm_x(jnp.concatenate([als_ref[b, :, sls[p]], rs_ref[b, :, sls[p]]], axis=0), s_ps[i], NT,
               pb=PASSES_STATE) for i, (b, p) in enumerate(pairs)]
    tg = [jnp.concatenate([tg_ref[b, :, h * 2 * t:h * 2 * t + t], tg_ref[b, :, h * 2 * t + t:(h + 1) * 2 * t]],
                          axis=0) for b, h in units]
    res = [_mm_x(tg[u], jnp.where(lo if h % 2 == 0 else ~lo, x[2 * b + h // 2][:t], zero), pb=PASSES_STATE)
           for u, (b, h) in enumerate(units)]
    tx = [res[GROUP_HEADS * b + 2 * p][:t] + res[GROUP_HEADS * b + 2 * p + 1][:t] for b, p in pairs]
    gx = [res[GROUP_HEADS * b + 2 * p][t:] + res[GROUP_HEADS * b + 2 * p + 1][t:] for b, p in pairs]
    upd = [_mm_x(tx[i], bh_ref[b, :, sls[p]], TN, pa=PASSES_STATE) for i, (b, p) in enumerate(pairs)]
    for i, (b, p) in enumerate(pairs):
        s_scr[b, p] = (s_ps[i] * wc_ref[b, 0][0:1, sls[p]] + sconst_ref[b, :, sls[p]]
                       + jnp.where(blockdiag, upd[i], zero))
    for b in range(nb):
        y_tiles = [x[2 * b + p][t:] + gx[2 * b + p] + yind_ref[b, :, sls[p]] for p in range(2)]
        y = _head_rms(jnp.concatenate(y_tiles, axis=1), g_ref[...]) + bonus_ref[b]
        out_ref[b] = (y * gate_ref[b]).astype(out_ref.dtype)


def _rwkv_scan(prep, g, bsz, nb=4):
    als, rs, bh, tg, _uind, yind, bonus, gate, sconst, wc = prep
    m = als.shape[0]
    nb = min(nb, bsz)
    seq = m // bsz
    nc = seq // CHUNK
    per_batch = lambda a: a.reshape((bsz, seq) + a.shape[1:])
    wide = pl.BlockSpec((nb, CHUNK, GROUP_W), lambda b, c: (b, c, 0))
    out = pl.pallas_call(
        _rwkv_scan_kernel,
        grid=(bsz // nb, nc),
        in_specs=[wide, wide, wide,
                  pl.BlockSpec((nb, CHUNK, 4 * GROUP_W), lambda b, c: (b, c, 0)),
                  wide, wide, wide, wide,
                  pl.BlockSpec((nb, 1, 8, GROUP_W), lambda b, c: (b, c, 0, 0)),
                  pl.BlockSpec(g.shape, lambda b, c: (0, 0))],
        out_specs=wide,
        out_shape=jax.ShapeDtypeStruct((bsz, seq, GROUP_W), BF16),
        scratch_shapes=[pltpu.VMEM((nb, 2, LANES, LANES), F32)],
        compiler_params=_cparams(("parallel", "arbitrary")),
        name="rwkv_scan",
    )(per_batch(als), per_batch(rs), per_batch(bh), per_batch(tg), per_batch(yind), per_batch(bonus),
      per_batch(gate), per_batch(sconst), wc.reshape(bsz, nc, 8, GROUP_W), g)
    return out.reshape(m, GROUP_W)


def _row(v, width=None, offset=0):
    v = v.astype(F32).reshape(-1)
    width = v.shape[0] if width is None else width
    return jnp.zeros((1, width), F32).at[0, offset:offset + v.shape[0]].set(v)


def _w_in_segments():
    gw = GROUP_W
    b0 = 3 * gw
    c0 = b0 + 4 * gw + GROUP_HEADS
    d0 = c0 + 5 * gw + 2 * GROUP_HEADS
    m0 = d0 + 4 * gw + DECAY_RANK + A_RANK
    wide = [0, gw, 2 * gw,
            b0, b0 + gw, b0 + 2 * gw, b0 + 3 * gw + GROUP_HEADS,
            c0, c0 + gw, c0 + 2 * gw, c0 + 3 * gw + 2 * GROUP_HEADS, c0 + 4 * gw + 2 * GROUP_HEADS,
            d0, d0 + gw + DECAY_RANK, d0 + 2 * gw + DECAY_RANK, d0 + 3 * gw + DECAY_RANK + A_RANK,
            m0, m0 + gw]
    narrow = [(b0 + 3 * gw, GROUP_HEADS, SM_BF), (c0 + 3 * gw, 2 * GROUP_HEADS, SM_CI),
              (d0 + gw, DECAY_RANK, SM_WLO), (d0 + 3 * gw + DECAY_RANK, A_RANK, SM_ALO)]
    return wide, narrow, m0 + 2 * gw


def _wlayout_kernel(w_ref, o_ref):
    wide, narrow, _ = _w_in_segments()
    rows = w_ref.shape[0]
    for seg, src in enumerate(wide):
        for half in range(GROUP_W // LANES):
            s = src + half * LANES
            base, off = (s // LANES) * LANES, s % LANES
            win = w_ref[:, base:base + 2 * LANES]
            blk = win[:, :LANES] if off == 0 else pltpu.roll(win, 2 * LANES - off, axis=1)[:, :LANES]
            d = seg * GROUP_W + half * LANES
            o_ref[:, d:d + LANES] = blk.astype(o_ref.dtype)
    lane = _iota((rows, LANES), 1)
    small = jnp.zeros((rows, LANES), F32)
    for src, width, dst in narrow:
        base, off = (src // LANES) * LANES, src % LANES
        win = w_ref[:, base:base + LANES]
        if dst != off:
            win = pltpu.roll(win, (dst - off) % LANES, axis=1)
        small = jnp.where((lane >= dst) & (lane < dst + width), win, small)
    o_ref[:, N_WIDE:] = small.astype(o_ref.dtype)


def _layout_w_in(w_in, rows=256):
    d, n_src = w_in.shape
    assert n_src == _w_in_segments()[2]
    w_pad = jnp.pad(w_in, ((0, 0), (0, NP + LANES - n_src)))
    return pl.pallas_call(
        _wlayout_kernel,
        grid=(d // rows,),
        in_specs=[pl.BlockSpec((rows, NP + LANES), lambda i: (i, 0))],
        out_specs=pl.BlockSpec((rows, NP), lambda i: (i, 0)),
        out_shape=jax.ShapeDtypeStruct((d, NP), BF16),
        compiler_params=_cparams(("parallel",)),
        name="wlayout",
    )(w_pad)


def _layout_mu(mu):
    gw = GROUP_W
    o_w, o_k, o_v, o_a, o_z = gw, gw + DECAY_RANK, 2 * gw + DECAY_RANK, 3 * gw + DECAY_RANK, 3 * gw + DECAY_RANK + A_RANK
    mu_s = jnp.zeros((1, LANES), F32)
    mu_s = mu_s.at[0, SM_WLO:SM_WLO + DECAY_RANK].set(mu[o_w:o_w + DECAY_RANK])
    mu_s = mu_s.at[0, SM_ALO:SM_ALO + A_RANK].set(mu[o_a:o_a + A_RANK])
    return dict(mu_r=_row(mu[0:gw]), mu_k=_row(mu[o_k:o_k + gw]), mu_v=_row(mu[o_v:o_v + gw]),
                mu_z=_row(mu[o_z:o_z + gw]), mu_s=mu_s)


def _pad_rows(w, offset):
    return jnp.zeros((LANES, w.shape[1]), F32).at[offset:offset + w.shape[0]].set(w).astype(BF16)


def _layer(x2, mem2, bsz, n_mem, norm_g, w_in, w_out, sgu_norm_g, sgu_w, sgu_b, fox_q_g, fox_k_g, fox_f_b,
           mlstm_conv_w, mlstm_conv_b, mlstm_i_b, mlstm_f_b, mlstm_out_g,
           rwkv_mu, rwkv_w0, rwkv_w2, rwkv_a0, rwkv_a2, rwkv_k_k, rwkv_k_a, rwkv_r_k, rwkv_ln_g,
           mem_norm_g, mem_w_kv, mem_q_g, mem_k_g):
    tile_h = lambda g: _row(jnp.tile(g, GROUP_HEADS))
    proj = _inproj(x2, _row(norm_g), _layout_w_in(w_in))

    ya = _sgu(proj, _row(sgu_norm_g), sgu_w, jnp.repeat(sgu_b.T, HEAD_DIM, axis=1))

    kaug, qaug_t, v_t = _foxprep(proj, tile_h(fox_q_g), tile_h(fox_k_g), _row(fox_f_b, LANES, SM_BF), bsz)
    yb = _fox(proj, kaug, qaug_t, v_t, bsz)

    cw = jnp.zeros((8, 2 * GROUP_W), F32).at[:CONV_K].set(mlstm_conv_w)
    yc = _mlstm(proj, cw, _row(mlstm_conv_b), _row(mlstm_i_b, LANES, SM_CI), _row(mlstm_f_b, LANES, SM_CF),
                _row(mlstm_out_g), bsz)

    prm = _layout_mu(rwkv_mu)
    prm.update(w0=_row(rwkv_w0), w2p=_pad_rows(rwkv_w2, SM_WLO), a0=_row(rwkv_a0), a2p=_pad_rows(rwkv_a2, SM_ALO),
               k_k=_row(rwkv_k_k), k_a=_row(rwkv_k_a), r_k=_row(rwkv_r_k))
    yd = _rwkv_scan(_rwkv_prep(proj, prm, bsz), _row(rwkv_ln_g), bsz)

    mk, mv = _memkv(mem2, _row(mem_norm_g), mem_w_kv.astype(BF16), tile_h(mem_k_g), n_mem)
    ym = _memattn(proj, mk, mv, tile_h(mem_q_g), bsz, n_mem)

    return _outproj([ya, yb, yc, yd, ym], w_out.astype(BF16), x2)


def kernel(x, mem, norm_g, w_in, w_out, sgu_norm_g, sgu_w, sgu_b, fox_q_g, fox_k_g, fox_f_b, mlstm_conv_w, mlstm_conv_b, mlstm_i_b, mlstm_f_b, mlstm_out_g, rwkv_mu, rwkv_w0, rwkv_w2, rwkv_a0, rwkv_a2, rwkv_k_k, rwkv_k_a, rwkv_r_k, rwkv_ln_g, mem_norm_g, mem_w_kv, mem_q_g, mem_k_g):
    bsz, seq, d = x.shape
    n_mem = mem.shape[1]
    stacked = (norm_g, w_in, w_out, sgu_norm_g, sgu_w, sgu_b, fox_q_g, fox_k_g, fox_f_b,
               mlstm_conv_w, mlstm_conv_b, mlstm_i_b, mlstm_f_b, mlstm_out_g,
               rwkv_mu, rwkv_w0, rwkv_w2, rwkv_a0, rwkv_a2, rwkv_k_k, rwkv_k_a, rwkv_r_k, rwkv_ln_g,
               mem_norm_g, mem_w_kv, mem_q_g, mem_k_g)
    x2 = x.reshape(bsz * seq, d)
    mem2 = mem.reshape(bsz * n_mem, d)
    for layer in range(norm_g.shape[0]):
        x2 = _layer(x2, mem2, bsz, n_mem, *[p[layer] for p in stacked])
    return x2.reshape(bsz, seq, d)
```
